```python
import jax, jax.numpy as jnp
from jax import lax
import numpy as np

D_MODEL = 1024
BATCH = 2
SEQ = 8192
DEPTH = 4

GRID_W = 64
CTX_LEN = 256
N_MIXERS = 2
D_INNER = D_MODEL
HEAD_DIM = 64
N_HEADS = D_INNER // HEAD_DIM
N_FGROUPS = 8
FGROUP_DIM = D_INNER // N_FGROUPS
LORA_DECAY = 64
LORA_ICLR = 64
LORA_VRES = 32
N_DIR = 2
N_TSHIFT = 6
N_FOURIER = (DEPTH + 1) // 2
N_RWKV = DEPTH // 2
RMS_EPS = 1e-6
GN_EPS = 64e-5

kernel_name = "fnet_rwkv7_sandwich_adaln_prefix_trunk"


def rmsnorm(x, g):
    x32 = x.astype(jnp.float32)
    y = x32 * lax.rsqrt(jnp.mean(x32 * x32, axis=-1, keepdims=True) + RMS_EPS)
    return (y * g.astype(jnp.float32)).astype(x.dtype)


def modulation(cond, w, b):
    return jnp.split(jax.nn.silu(cond) @ w + b, 3, axis=-1)


def shift_grid(h):
    b_, t, d = h.shape
    g = h.reshape(b_, t // GRID_W, GRID_W, d)
    p = jnp.pad(g, ((0, 0), (1, 1), (1, 1), (0, 0)))
    nb = (p[:, :-2, 1:-1] + p[:, 2:, 1:-1] + p[:, 1:-1, :-2] + p[:, 1:-1, 2:]) * 0.25
    return nb.reshape(b_, t, d)


def shift_seq(h):
    p = jnp.pad(h, ((0, 0), (1, 1), (0, 0)))
    return (p[:, :-2] + p[:, 2:]) * 0.5


def fourier_mixer(h, w_in, w_mix, b_mix, w_out):
    b_, t, _ = h.shape
    u, z = jnp.split(h @ w_in, 2, axis=-1)
    ug = u.reshape(b_, t, N_FGROUPS, FGROUP_DIM).astype(jnp.float32)
    f = jnp.fft.fftn(ug, axes=(1, 3), norm="ortho").real.astype(h.dtype)
    f = jnp.einsum('btgc,gce->btge', f, w_mix).reshape(b_, t, D_INNER) + b_mix
    return (f * jax.nn.silu(z)) @ w_out


def rwkv_features(h, shift_fn, mu, w_in, w0, w1, w2, a0, a1, a2, k_k, k_a, v_res):
    heads = lambda t: t.reshape(t.shape[:-1] + (N_HEADS, HEAD_DIM))
    xs = h[None] + (shift_fn(h) - h)[None] * mu[:, None, None, :]
    r, k, v, gz = jnp.einsum('pbtd,pde->pbte', xs[:4], w_in)
    xv, xw, xa = xs[2], xs[4], xs[5]
    if v_res is not None:
        v_first, v0, v1, v2 = v_res
        v = v + (v_first - v) * jax.nn.sigmoid(v0 + (xv @ v1) @ v2)
    kk = heads(k * k_k).astype(jnp.float32)
    kk = kk / jnp.maximum(jnp.linalg.norm(kk, axis=-1, keepdims=True), 1e-12)
    wz = w0[:, None, None, :] + jnp.einsum('nbtl,nle->nbte', jnp.tanh(jnp.einsum('btd,ndl->nbtl', xw, w1)), w2)
    decay = jnp.exp(-jnp.exp(-jax.nn.softplus(-wz.astype(jnp.float32)) - 0.5))
    a = jax.nn.sigmoid(a0[:, None, None, :] + jnp.einsum('nbtl,nle->nbte', jnp.einsum('btd,ndl->nbtl', xa, a1), a2))
    k_dir = k[None] * (1 + (a - 1) * k_a)
    b = kk[None] * heads(a)
    return heads(r), heads(k_dir), heads(v), kk, heads(decay), b, gz, v


def wkv_scan(state, r, w, k, v, a, b, reverse):
    def step(s, inp):
        r_t, w_t, k_t, v_t, a_t, b_t = inp
        sa = jnp.einsum('bhij,bhj->bhi', s, a_t)
        s = s * w_t[:, :, None, :] + sa[..., None] * b_t[:, :, None, :] + v_t[..., None] * k_t[:, :, None, :]
        return s, jnp.einsum('bhij,bhj->bhi', s, r_t)
    tm = lambda t: jnp.swapaxes(t, 0, 1).astype(jnp.float32)
    state, y = lax.scan(step, state, tuple(tm(t) for t in (r, w, k, v, a, b)), reverse=reverse)
    return state, jnp.swapaxes(y, 0, 1)


def run_scan(state, feats, n, reverse):
    r, k_dir, v, kk, decay, b = feats[:6]
    return wkv_scan(state, r, decay[n], k_dir[n], v, -kk, b[n], reverse)


def rwkv_output(y, feats, r_k, ln_w, ln_b, w_out):
    r, k_dir, v, gz = feats[0], feats[1], feats[2], feats[6]
    b_, t = y.shape[:2]
    mean = jnp.mean(y, axis=-1, keepdims=True)
    var = jnp.mean(jnp.square(y - mean), axis=-1, keepdims=True)
    yn = ((y - mean) * lax.rsqrt(var + GN_EPS)).reshape(b_, t, D_INNER).astype(gz.dtype) * ln_w + ln_b
    bonus = jnp.sum(jnp.sum(r[None] * k_dir * r_k, axis=-1, keepdims=True), axis=0) * v
    return ((yn + bonus.reshape(b_, t, D_INNER)) * jax.nn.silu(gz)) @ w_out


def rwkv_mixer(h_ctx, h_lat, mu, w_in, w0, w1, w2, a0, a1, a2, k_k, k_a, r_k, ln_w, ln_b, w_out,
               v_res_ctx, v_res_lat, ctx_out):
    fc = rwkv_features(h_ctx, shift_seq, mu, w_in, w0, w1, w2, a0, a1, a2, k_k, k_a, v_res_ctx)
    fl = rwkv_features(h_lat, shift_grid, mu, w_in, w0, w1, w2, a0, a1, a2, k_k, k_a, v_res_lat)
    state0 = jnp.zeros((h_lat.shape[0], N_HEADS, HEAD_DIM, HEAD_DIM), jnp.float32)
    y_ctx = 0.0
    y_lat = 0.0
    for n, rev in enumerate((False, True)):
        s_ctx, yc = run_scan(state0, fc, n, rev)
        _, yl = run_scan(s_ctx, fl, n, rev)
        y_ctx = y_ctx + yc
        y_lat = y_lat + yl
    out_lat = rwkv_output(y_lat, fl, r_k, ln_w, ln_b, w_out)
    out_ctx = rwkv_output(y_ctx, fc, r_k, ln_w, ln_b, w_out) if ctx_out else None
    return out_ctx, out_lat, fc[7], fl[7]


def setup_inputs(seed: int = 0) -> dict:
    key = jax.random.key(seed)
    ks = iter(jax.random.split(key, 40))
    nrm = lambda shape, s: jax.random.normal(next(ks), shape, jnp.float32) * s
    nv = max(N_RWKV - 1, 0)
    return {
        "x": nrm((BATCH, SEQ, D_MODEL), 1.0),
        "c": nrm((BATCH, D_MODEL), 1.0),
        "ctx": nrm((BATCH, CTX_LEN, D_MODEL), 1.0),
        "c_ctx": nrm((D_MODEL,), 1.0),
        "mod_w": nrm((DEPTH, D_MODEL, 3 * D_MODEL), D_MODEL ** -0.5),
        "mod_b": nrm((DEPTH, 3 * D_MODEL), 0.02),
        "norm_pre": 1.0 + nrm((DEPTH, D_MODEL), 0.05),
        "norm_post": 1.0 + nrm((DEPTH, D_MODEL), 0.05),
        "f_w_in": nrm((N_FOURIER, D_MODEL, 2 * D_INNER), D_MODEL ** -0.5),
        "f_w_mix": nrm((N_FOURIER, N_FGROUPS, FGROUP_DIM, FGROUP_DIM), FGROUP_DIM ** -0.5),
        "f_b_mix": nrm((N_FOURIER, D_INNER), 0.02),
        "f_w_out": nrm((N_FOURIER, D_INNER, D_MODEL), D_INNER ** -0.5),
        "r_mu": jax.random.uniform(next(ks), (N_RWKV, N_TSHIFT, D_MODEL), jnp.float32),
        "r_w_in": nrm((N_RWKV, 4, D_MODEL, D_INNER), D_MODEL ** -0.5),
        "r_w0": jax.random.uniform(next(ks), (N_RWKV, N_DIR, D_INNER), jnp.float32, minval=-3.0, maxval=0.5),
        "r_w1": nrm((N_RWKV, N_DIR, D_MODEL, LORA_DECAY), D_MODEL ** -0.5),
        "r_w2": nrm((N_RWKV, N_DIR, LORA_DECAY, D_INNER), 0.5 * LORA_DECAY ** -0.5),
        "r_a0": nrm((N_RWKV, N_DIR, D_INNER), 0.5),
        "r_a1": nrm((N_RWKV, N_DIR, D_MODEL, LORA_ICLR), D_MODEL ** -0.5),
        "r_a2": nrm((N_RWKV, N_DIR, LORA_ICLR, D_INNER), LORA_ICLR ** -0.5),
        "r_v0": nrm((nv, D_INNER), 0.5),
        "r_v1": nrm((nv, D_MODEL, LORA_VRES), D_MODEL ** -0.5),
        "r_v2": nrm((nv, LORA_VRES, D_INNER), LORA_VRES ** -0.5),
        "r_k_k": 1.0 + nrm((N_RWKV, D_INNER), 0.1),
        "r_k_a": 1.0 + nrm((N_RWKV, D_INNER), 0.1),
        "r_r_k": nrm((N_RWKV, N_HEADS, HEAD_DIM), 0.1),
        "r_ln_w": 1.0 + nrm((N_RWKV, D_INNER), 0.05),
        "r_ln_b": nrm((N_RWKV, D_INNER), 0.01),
        "r_w_out": nrm((N_RWKV, D_INNER, D_MODEL), D_INNER ** -0.5),
    }


def reference(x, c, ctx, c_ctx, mod_w, mod_b, norm_pre, norm_post, f_w_in, f_w_mix, f_b_mix, f_w_out,
              r_mu, r_w_in, r_w0, r_w1, r_w2, r_a0, r_a1, r_a2, r_v0, r_v1, r_v2,
              r_k_k, r_k_a, r_r_k, r_ln_w, r_ln_b, r_w_out):
    cond_lat = c[:, None, :]
    cond_ctx = c_ctx[None, None, :]
    v_first_ctx = None
    v_first_lat = None
    for i in range(DEPTH):
        last = i == DEPTH - 1
        kind = i % N_MIXERS
        j = i // N_MIXERS
        need_ctx = (not last) or kind == 1
        sh, sc, gt = modulation(cond_lat, mod_w[i], mod_b[i])
        h_lat = rmsnorm(x, norm_pre[i]) * (1 + sc) + sh
        if need_ctx:
            sh_c, sc_c, gt_c = modulation(cond_ctx, mod_w[i], mod_b[i])
            h_ctx = rmsnorm(ctx, norm_pre[i]) * (1 + sc_c) + sh_c
        if kind == 0:
            o_lat = fourier_mixer(h_lat, f_w_in[j], f_w_mix[j], f_b_mix[j], f_w_out[j])
            o_ctx = None if last else fourier_mixer(h_ctx, f_w_in[j], f_w_mix[j], f_b_mix[j], f_w_out[j])
        else:
            if j == 0:
                vres_c = None
                vres_l = None
            else:
                vres_c = (v_first_ctx, r_v0[j - 1], r_v1[j - 1], r_v2[j - 1])
                vres_l = (v_first_lat, r_v0[j - 1], r_v1[j - 1], r_v2[j - 1])
            o_ctx, o_lat, v_c, v_l = rwkv_mixer(
                h_ctx, h_lat, r_mu[j], r_w_in[j], r_w0[j], r_w1[j], r_w2[j], r_a0[j], r_a1[j], r_a2[j],
                r_k_k[j], r_k_a[j], r_r_k[j], r_ln_w[j], r_ln_b[j], r_w_out[j], vres_c, vres_l, not last)
            if j == 0:
                v_first_ctx = v_c
                v_first_lat = v_l
        x = x + gt * rmsnorm(o_lat, norm_post[i])
        if not last:
            ctx = ctx + gt_c * rmsnorm(o_ctx, norm_post[i])
    return x
```

```python
import functools
import math

import numpy as np
import jax
import jax.numpy as jnp
from jax import lax
from jax.experimental import pallas as pl
from jax.experimental.pallas import tpu as pltpu

F32 = jnp.float32
BF16 = jnp.bfloat16
HIGHEST = lax.Precision.HIGHEST

D_MODEL = 1024
DEPTH = 4
GRID_W = 64
HEAD_DIM = 64
N_HEADS = D_MODEL // HEAD_DIM
N_FGROUPS = 8
FGROUP_DIM = D_MODEL // N_FGROUPS
LORA_DECAY = 64
LORA_ICLR = 64
LORA_VRES = 32
RMS_EPS = 1e-6
GN_EPS = 64e-5

LANES = 128
CHUNK = 64
HEADS_PER_TILE = LANES // HEAD_DIM
N_PAIRS = D_MODEL // LANES
VMEM_LIMIT = 56 * 1024 * 1024


def _cparams(sem):
    return pltpu.CompilerParams(dimension_semantics=sem, vmem_limit_bytes=VMEM_LIMIT)


def _dot(a, b):
    return jnp.dot(a, b, preferred_element_type=F32)


def _dot_hi(a, b):
    return jnp.dot(a, b, preferred_element_type=F32, precision=HIGHEST)


def _dot_nt_hi(a, b):
    return lax.dot_general(a, b, (((1,), (1,)), ((), ())), preferred_element_type=F32, precision=HIGHEST)


def _dot_tn_hi(a, b):
    return lax.dot_general(a, b, (((0,), (0,)), ((), ())), preferred_element_type=F32, precision=HIGHEST)


def _rmsnorm(x, g):
    return x * lax.rsqrt(jnp.mean(x * x, axis=-1, keepdims=True) + RMS_EPS) * g


def _split_bf16(x):
    hi = x.astype(BF16)
    lo = (x - hi.astype(F32)).astype(BF16)
    return hi, lo


def _seg_sum(x, e):
    hi, lo = _split_bf16(x)
    return _dot(hi, e) + _dot(lo, e)


def _seg_bcast(s, et):
    hi, lo = _split_bf16(s)
    return _dot(hi, et) + _dot(lo, et)


def _head_indicator():
    e = np.zeros((D_MODEL, LANES), np.float32)
    e[np.arange(D_MODEL), np.arange(D_MODEL) // HEAD_DIM] = 1.0
    return jnp.asarray(e, BF16), jnp.asarray(e.T, BF16)


def _mod_kernel(cond_ref, w_ref, b_ref, o_ref):
    cond = cond_ref[...]
    s = cond * jax.nn.sigmoid(cond)
    o_ref[0] = _dot_hi(s, w_ref[0]) + b_ref[0]


def _modulation(cond, mod_w, mod_b):
    rows = cond.shape[0]
    tn = 1536
    return pl.pallas_call(
        _mod_kernel,
        grid=(DEPTH, 3 * D_MODEL // tn),
        in_specs=[
            pl.BlockSpec((rows, D_MODEL), lambda i, j: (0, 0)),
            pl.BlockSpec((1, D_MODEL, tn), lambda i, j: (i, 0, j)),
            pl.BlockSpec((1, 1, tn), lambda i, j: (i, 0, j)),
        ],
        out_specs=pl.BlockSpec((1, rows, tn), lambda i, j: (i, 0, j)),
        out_shape=jax.ShapeDtypeStruct((DEPTH, rows, 3 * D_MODEL), F32),
        compiler_params=_cparams(("parallel", "parallel")),
        name="modulation",
    )(cond, mod_w, mod_b.reshape(DEPTH, 1, 3 * D_MODEL))


def _wcs_kernel(cc_ref, sc_ref, w_ref, o_ref):
    w = w_ref[0]
    o_ref[0, :, :FGROUP_DIM] = _dot_hi(cc_ref[...], w).astype(o_ref.dtype)
    o_ref[0, :, FGROUP_DIM:] = _dot_hi(sc_ref[...], w).astype(o_ref.dtype)


def _fold_channel_dft(w_mix):
    n = np.arange(FGROUP_DIM)
    ang = 2.0 * np.pi * ((n[:, None] * n[None, :]) % FGROUP_DIM) / FGROUP_DIM
    cc = jnp.asarray(np.cos(ang), F32)
    sc = jnp.asarray(np.sin(ang), F32)
    return pl.pallas_call(
        _wcs_kernel,
        grid=(N_FGROUPS,),
        in_specs=[
            pl.BlockSpec((FGROUP_DIM, FGROUP_DIM), lambda g: (0, 0)),
            pl.BlockSpec((FGROUP_DIM, FGROUP_DIM), lambda g: (0, 0)),
            pl.BlockSpec((1, FGROUP_DIM, FGROUP_DIM), lambda g: (g, 0, 0)),
        ],
        out_specs=pl.BlockSpec((1, FGROUP_DIM, 2 * FGROUP_DIM), lambda g: (g, 0, 0)),
        out_shape=jax.ShapeDtypeStruct((N_FGROUPS, FGROUP_DIM, 2 * FGROUP_DIM), BF16),
        compiler_params=_cparams(("parallel",)),
        name="fold_channel_dft",
    )(cc, sc, w_mix)


def _dft_table_kernel(ca_ref, sa_ref, cb_ref, sb_ref, c_ref, s_ref):
    ca = ca_ref[0]
    sa = sa_ref[0]
    cb = cb_ref[...]
    sb = sb_ref[...]
    c_ref[...] = (ca * cb - sa * sb).astype(c_ref.dtype)
    s_ref[...] = (sa * cb + ca * sb).astype(s_ref.dtype)


def _angles(p, t, period):
    return (2.0 * math.pi / period) * ((p[:, None] * t[None, :]) % period).astype(F32)


def _dft_tables(t_len):
    tp = min(256, t_len)
    tt = min(2048, t_len)
    nb = t_len // tp
    t = jnp.arange(t_len, dtype=jnp.int32)
    ang_a = _angles(jnp.arange(nb, dtype=jnp.int32) * tp, t, t_len)
    ang_b = _angles(jnp.arange(tp, dtype=jnp.int32), t, t_len)
    ca = jnp.cos(ang_a).reshape(nb, 1, t_len)
    sa = jnp.sin(ang_a).reshape(nb, 1, t_len)
    cb = jnp.cos(ang_b)
    sb = jnp.sin(ang_b)
    return pl.pallas_call(
        _dft_table_kernel,
        grid=(t_len // tt, nb),
        in_specs=[
            pl.BlockSpec((1, 1, tt), lambda j, i: (i, 0, j)),
            pl.BlockSpec((1, 1, tt), lambda j, i: (i, 0, j)),
            pl.BlockSpec((tp, tt), lambda j, i: (0, j)),
            pl.BlockSpec((tp, tt), lambda j, i: (0, j)),
        ],
        out_specs=[
            pl.BlockSpec((tp, tt), lambda j, i: (i, j)),
            pl.BlockSpec((tp, tt), lambda j, i: (i, j)),
        ],
        out_shape=[jax.ShapeDtypeStruct((t_len, t_len), BF16)] * 2,
        compiler_params=_cparams(("parallel", "parallel")),
        name="dft_tables",
    )(ca, sa, cb, sb)


def _fourier_in_kernel(x_ref, g_ref, sc_ref, sh_ref, win_ref, wcs_ref, p_ref, sz_ref):
    h = _rmsnorm(x_ref[...], g_ref[...]) * (1.0 + sc_ref[0]) + sh_ref[0]
    uz = _dot(h.astype(BF16), win_ref[...])
    z = uz[:, D_MODEL:]
    sz_ref[...] = (z * jax.nn.sigmoid(z)).astype(sz_ref.dtype)
    for g in range(N_FGROUPS):
        lo, hi = g * FGROUP_DIM, (g + 1) * FGROUP_DIM
        pg = _dot(uz[:, lo:hi].astype(BF16), wcs_ref[g])
        p_ref[:, lo:hi] = pg[:, :FGROUP_DIM].astype(p_ref.dtype)
        p_ref[:, D_MODEL + lo:D_MODEL + hi] = pg[:, FGROUP_DIM:].astype(p_ref.dtype)


def _fourier_in(x, t_len, g_pre, sc, sh, w_in, wcs):
    rows = x.shape[0]
    tm = min(512, t_len)
    tpb = t_len // tm
    nbm = sc.shape[0]
    modmap = (lambda i: (i // tpb, 0, 0)) if nbm > 1 else (lambda i: (0, 0, 0))
    return pl.pallas_call(
        _fourier_in_kernel,
        grid=(rows // tm,),
        in_specs=[
            pl.BlockSpec((tm, D_MODEL), lambda i: (i, 0)),
            pl.BlockSpec((1, D_MODEL), lambda i: (0, 0)),
            pl.BlockSpec((1, 1, D_MODEL), modmap),
            pl.BlockSpec((1, 1, D_MODEL), modmap),
            pl.BlockSpec((D_MODEL, 2 * D_MODEL), lambda i: (0, 0)),
            pl.BlockSpec((N_FGROUPS, FGROUP_DIM, 2 * FGROUP_DIM), lambda i: (0, 0, 0)),
        ],
        out_specs=[
            pl.BlockSpec((tm, 2 * D_MODEL), lambda i: (i, 0)),
            pl.BlockSpec((tm, D_MODEL), lambda i: (i, 0)),
        ],
        out_shape=[
            jax.ShapeDtypeStruct((rows, 2 * D_MODEL), BF16),
            jax.ShapeDtypeStruct((rows, D_MODEL), BF16),
        ],
        compiler_params=_cparams(("parallel",)),
        name="fourier_in",
    )(x, g_pre, sc, sh, w_in, wcs)


def _fourier_out_kernel(c_ref, s_ref, pc_ref, ps_ref, sz_ref, bmix_ref, wout_ref, x_ref, gpost_ref, gt_ref,
                        o_ref, acc_ref, *, scale):
    k = pl.program_id(2)

    @pl.when(k == 0)
    def _():
        acc_ref[...] = jnp.zeros_like(acc_ref)

    acc_ref[...] += _dot(c_ref[...], pc_ref[...]) - _dot(s_ref[...], ps_ref[...])

    @pl.when(k == pl.num_programs(2) - 1)
    def _():
        f = acc_ref[...] * scale + bmix_ref[...]
        gated = (f * sz_ref[...].astype(F32)).astype(BF16)
        o = _dot(gated, wout_ref[...])
        o_ref[...] = x_ref[...] + gt_ref[0] * _rmsnorm(o, gpost_ref[...])


def _fourier_out(c_tab, s_tab, p, sz, b_mix, w_out, x, g_post, gt, t_len):
    rows = x.shape[0]
    nb = rows // t_len
    tm = min(512, t_len)
    tk = min(1024, t_len)
    nti, ntk = t_len // tm, t_len // tk
    nbm = gt.shape[0]
    modmap = (lambda b, i, k: (b, 0, 0)) if nbm > 1 else (lambda b, i, k: (0, 0, 0))
    scale = 1.0 / math.sqrt(t_len * FGROUP_DIM)
    return pl.pallas_call(
        functools.partial(_fourier_out_kernel, scale=scale),
        grid=(nb, nti, ntk),
        in_specs=[
            pl.BlockSpec((tm, tk), lambda b, i, k: (i, k)),
            pl.BlockSpec((tm, tk), lambda b, i, k: (i, k)),
            pl.BlockSpec((tk, D_MODEL), lambda b, i, k: (b * ntk + k, 0)),
            pl.BlockSpec((tk, D_MODEL), lambda b, i, k: (b * ntk + k, 1)),
            pl.BlockSpec((tm, D_MODEL), lambda b, i, k: (b * nti + i, 0)),
            pl.BlockSpec((1, D_MODEL), lambda b, i, k: (0, 0)),
            pl.BlockSpec((D_MODEL, D_MODEL), lambda b, i, k: (0, 0)),
            pl.BlockSpec((tm, D_MODEL), lambda b, i, k: (b * nti + i, 0)),
            pl.BlockSpec((1, D_MODEL), lambda b, i, k: (0, 0)),
            pl.BlockSpec((1, 1, D_MODEL), modmap),
        ],
        out_specs=pl.BlockSpec((tm, D_MODEL), lambda b, i, k: (b * nti + i, 0)),
        out_shape=jax.ShapeDtypeStruct((rows, D_MODEL), F32),
        scratch_shapes=[pltpu.VMEM((tm, D_MODEL), F32)],
        compiler_params=_cparams(("parallel", "parallel", "arbitrary")),
        name="fourier_out",
    )(c_tab, s_tab, p, p, sz, b_mix, w_out, x, g_post, gt)


def _rwkv_feat_kernel(*refs, grid_shift, tiles_per_seq, has_vres):
    if grid_shift:
        x_ref, xp_ref, xn_ref = refs[:3]
        refs = refs[3:]
    else:
        x_ref = refs[0]
        refs = refs[1:]
    (g_ref, sc_ref, sh_ref, mu_ref, w4_ref, w1_ref, w2_ref, w0_ref, a1_ref, a2_ref, a0_ref,
     kk_ref, ka_ref, rk_ref, e_ref, et_ref) = refs[:16]
    refs = refs[16:]
    if has_vres:
        vf_ref, v1_ref, v2_ref, v0_ref = refs[:4]
        refs = refs[4:]
    (r_out, v_out, kkn_out, k0_out, k1_out, lw0_out, lw1_out, b0_out, b1_out, g_out, bonus_out) = refs

    g = g_ref[...]
    sc1 = 1.0 + sc_ref[0]
    sh = sh_ref[0]
    tm = x_ref.shape[0]
    h = _rmsnorm(x_ref[...], g) * sc1 + sh
    row = lax.broadcasted_iota(jnp.int32, (tm, 1), 0)
    if grid_shift:
        i = pl.program_id(0)
        first = (i % tiles_per_seq) == 0
        last = (i % tiles_per_seq) == tiles_per_seq - 1
        hp = _rmsnorm(xp_ref[...], g) * sc1 + sh
        hn = _rmsnorm(xn_ref[...], g) * sc1 + sh
        hp = jnp.where(first, 0.0, hp)
        hn = jnp.where(last, 0.0, hn)
        hall = jnp.concatenate([hp, h, hn], axis=0)
        tot = tm + 2 * GRID_W
        up = hall[0:tm]
        down = hall[2 * GRID_W:2 * GRID_W + tm]
        left = pltpu.roll(hall, 1, axis=0)[GRID_W:GRID_W + tm]
        right = pltpu.roll(hall, tot - 1, axis=0)[GRID_W:GRID_W + tm]
        col = row % GRID_W
        left = jnp.where(col == 0, 0.0, left)
        right = jnp.where(col == GRID_W - 1, 0.0, right)
        nbr = (up + down + left + right) * 0.25
    else:
        left = jnp.where(row == 0, 0.0, pltpu.roll(h, 1, axis=0))
        right = jnp.where(row == tm - 1, 0.0, pltpu.roll(h, tm - 1, axis=0))
        nbr = (left + right) * 0.5
    dlt = nbr - h

    def mix(p):
        return h + dlt * mu_ref[p:p + 1, :]

    xr, xk, xv, xg, xw, xa = (mix(p).astype(BF16) for p in range(6))
    r = _dot(xr, w4_ref[0])
    k = _dot(xk, w4_ref[1])
    v = _dot(xv, w4_ref[2])
    gz = _dot(xg, w4_ref[3])
    if has_vres:
        vz = v0_ref[...] + _dot(_dot(xv, v1_ref[...]).astype(BF16), v2_ref[...])
        v = v + (vf_ref[...] - v) * jax.nn.sigmoid(vz)

    e = e_ref[...]
    et = et_ref[...]
    kk = k * kk_ref[...]
    n2 = _seg_sum(kk * kk, e)
    inv = 1.0 / jnp.maximum(jnp.sqrt(n2), 1e-12)
    kkn = kk * _seg_bcast(inv, et)

    tw = jnp.tanh(_dot(xw, w1_ref[...])).astype(BF16)
    ta = _dot(xa, a1_ref[...]).astype(BF16)
    decay_gain = -math.exp(-0.5)
    rsum = jnp.zeros_like(r)
    for n, (lw_out, k_out, b_out) in enumerate(((lw0_out, k0_out, b0_out), (lw1_out, k1_out, b1_out))):
        wz = w0_ref[n:n + 1, :] + _dot(tw[:, n * LORA_DECAY:(n + 1) * LORA_DECAY], w2_ref[n])
        lw_out[...] = decay_gain * jax.nn.sigmoid(wz)
        az = a0_ref[n:n + 1, :] + _dot(ta[:, n * LORA_ICLR:(n + 1) * LORA_ICLR], a2_ref[n])
        a = jax.nn.sigmoid(az)
        kd = k * (1.0 + (a - 1.0) * ka_ref[...])
        k_out[...] = kd
        b_out[...] = kkn * a
        rsum = rsum + r * kd * rk_ref[...]
    bonus_out[...] = _seg_bcast(_seg_sum(rsum, e), et) * v
    r_out[...] = r
    v_out[...] = v
    kkn_out[...] = kkn
    g_out[...] = (gz * jax.nn.sigmoid(gz)).astype(g_out.dtype)


def _rwkv_features(x, t_len, grid_shift, g_pre, sc, sh, wts, v_first):
    rows = x.shape[0]
    tm = 256
    tps = t_len // tm
    assert grid_shift or tps == 1
    nbm = sc.shape[0]
    modmap = (lambda i: (i // tps, 0, 0)) if nbm > 1 else (lambda i: (0, 0, 0))
    const2 = lambda i: (0, 0)
    const3 = lambda i: (0, 0, 0)
    rowmap = lambda i: (i, 0)
    has_vres = v_first is not None
    hb = tm // GRID_W
    nhalo = rows // GRID_W

    args, specs = [x], [pl.BlockSpec((tm, D_MODEL), rowmap)]
    if grid_shift:
        args += [x, x]
        specs += [
            pl.BlockSpec((GRID_W, D_MODEL), lambda i: (jnp.maximum(i * hb - 1, 0), 0)),
            pl.BlockSpec((GRID_W, D_MODEL), lambda i: (jnp.minimum((i + 1) * hb, nhalo - 1), 0)),
        ]
    args += [g_pre, sc, sh, wts["mu"], wts["w4"], wts["w1"], wts["w2"], wts["w0"], wts["a1"], wts["a2"], wts["a0"],
             wts["k_k"], wts["k_a"], wts["r_k"], wts["e"], wts["et"]]
    specs += [
        pl.BlockSpec((1, D_MODEL), const2),
        pl.BlockSpec((1, 1, D_MODEL), modmap),
        pl.BlockSpec((1, 1, D_MODEL), modmap),
        pl.BlockSpec((6, D_MODEL), const2),
        pl.BlockSpec((4, D_MODEL, D_MODEL), const3),
        pl.BlockSpec((D_MODEL, 2 * LORA_DECAY), const2),
        pl.BlockSpec((2, LORA_DECAY, D_MODEL), const3),
        pl.BlockSpec((2, D_MODEL), const2),
        pl.BlockSpec((D_MODEL, 2 * LORA_ICLR), const2),
        pl.BlockSpec((2, LORA_ICLR, D_MODEL), const3),
        pl.BlockSpec((2, D_MODEL), const2),
        pl.BlockSpec((1, D_MODEL), const2),
        pl.BlockSpec((1, D_MODEL), const2),
        pl.BlockSpec((1, D_MODEL), const2),
        pl.BlockSpec((D_MODEL, LANES), const2),
        pl.BlockSpec((LANES, D_MODEL), const2),
    ]
    if has_vres:
        args += [v_first, wts["v1"], wts["v2"], wts["v0"]]
        specs += [
            pl.BlockSpec((tm, D_MODEL), rowmap),
            pl.BlockSpec((D_MODEL, LORA_VRES), const2),
            pl.BlockSpec((LORA_VRES, D_MODEL), const2),
            pl.BlockSpec((1, D_MODEL), const2),
        ]
    n_out = 11
    out_dtypes = [F32] * 9 + [BF16, F32]
    return pl.pallas_call(
        functools.partial(_rwkv_feat_kernel, grid_shift=grid_shift, tiles_per_seq=tps, has_vres=has_vres),
        grid=(rows // tm,),
        in_specs=specs,
        out_specs=[pl.BlockSpec((tm, D_MODEL), rowmap)] * n_out,
        out_shape=[jax.ShapeDtypeStruct((rows, D_MODEL), dt) for dt in out_dtypes],
        compiler_params=_cparams(("parallel",)),
        name="rwkv_features",
    )(*args)


def _wkv_chunk(r, k, v, kkn, b, lw, s, rev):
    cl = CHUNK
    a = -kkn
    ti = lax.broadcasted_iota(jnp.int32, (cl, cl), 0)
    si = lax.broadcasted_iota(jnp.int32, (cl, cl), 1)
    tri = ((si >= ti) if rev else (si <= ti)).astype(F32)
    cum = _dot_hi(tri, lw)
    tot = jnp.sum(lw, axis=0, keepdims=True)
    e_in = jnp.exp(cum)
    e_ex = jnp.exp(cum - lw)
    e_neg = jnp.exp(-cum)
    e_rest = jnp.exp(tot - cum)
    w_all = jnp.exp(tot)

    lane = lax.broadcasted_iota(jnp.int32, (cl, LANES), 1)
    first_head = lane < HEAD_DIM

    def stack(x):
        return jnp.concatenate([jnp.where(first_head, x, 0.0), jnp.where(first_head, 0.0, x)], axis=0)

    a_t, r_t = stack(a * e_ex), stack(r * e_in)
    b_h, k_h = stack(b * e_neg), stack(k * e_neg)
    b_r, k_r = stack(b * e_rest), stack(k * e_rest)
    v_s = stack(v)

    scores = _dot_nt_hi(jnp.concatenate([a_t, r_t], axis=0), jnp.concatenate([b_h, k_h], axis=0))
    n2 = HEADS_PER_TILE * cl
    rt = lax.broadcasted_iota(jnp.int32, (n2, n2), 0) % cl
    ct = lax.broadcasted_iota(jnp.int32, (n2, n2), 1) % cl
    strict = (ct > rt) if rev else (ct < rt)
    incl = (ct >= rt) if rev else (ct <= rt)
    a_ab = jnp.where(strict, scores[:n2, :n2], 0.0)
    a_ak = jnp.where(strict, scores[:n2, n2:], 0.0)
    a_rb = jnp.where(incl, scores[n2:, :n2], 0.0)
    a_rk = jnp.where(incl, scores[n2:, n2:], 0.0)

    eye = (lax.broadcasted_iota(jnp.int32, (n2, n2), 0) == lax.broadcasted_iota(jnp.int32, (n2, n2), 1)).astype(F32)
    inv = eye + a_ab
    pw = a_ab
    for _ in range(int(math.log2(cl)) - 1):
        pw = _dot_hi(pw, pw)
        inv = inv + _dot_hi(inv, pw)

    x = _dot_nt_hi(a_t, s) + _dot_hi(a_ak, v_s)
    u = _dot_hi(inv, x)
    ys = _dot_nt_hi(r_t, s) + _dot_hi(a_rb, u) + _dot_hi(a_rk, v_s)
    y = ys[:cl] + ys[cl:]
    s_new = s * w_all + _dot_tn_hi(jnp.concatenate([u, v_s], axis=0), jnp.concatenate([b_r, k_r], axis=0))
    return y, s_new


def _wkv_kernel(rc, kc, vc, ac, bc, wc, rl, kl, vl, al, bl, wl, yc_ref, yl_ref, s_ref, *, rev, cpb):
    j = pl.program_id(2)

    def run(src, y_ref):
        r_ref, k_ref, v_ref, a_ref, b_ref, w_ref = src
        s = s_ref[...]
        order = range(cpb - 1, -1, -1) if rev else range(cpb)
        for c in order:
            sl = pl.ds(c * CHUNK, CHUNK)
            y, s = _wkv_chunk(r_ref[0, sl, :], k_ref[0, sl, :], v_ref[0, sl, :], a_ref[0, sl, :],
                              b_ref[0, sl, :], w_ref[0, sl, :], s, rev)
            y_ref[0, sl, :] = y
        s_ref[...] = s

    @pl.when(j == 0)
    def _():
        s_ref[...] = jnp.zeros_like(s_ref)
        run((rc, kc, vc, ac, bc, wc), yc_ref)

    @pl.when(j > 0)
    def _():
        run((rl, kl, vl, al, bl, wl), yl_ref)


def _wkv(ctx_ops, lat_ops, nb, t_ctx, t_lat, rev):
    bt = t_ctx
    cpb = bt // CHUNK
    nlb = t_lat // bt
    if rev:
        latmap = lambda b, p, j: (b, jnp.where(j == 0, nlb - 1, nlb - j), p)
    else:
        latmap = lambda b, p, j: (b, jnp.maximum(j - 1, 0), p)
    ctxmap = lambda b, p, j: (b, 0, p)
    blk = (1, bt, LANES)
    return pl.pallas_call(
        functools.partial(_wkv_kernel, rev=rev, cpb=cpb),
        grid=(nb, N_PAIRS, nlb + 1),
        in_specs=[pl.BlockSpec(blk, ctxmap)] * 6 + [pl.BlockSpec(blk, latmap)] * 6,
        out_specs=[pl.BlockSpec(blk, ctxmap), pl.BlockSpec(blk, latmap)],
        out_shape=[jax.ShapeDtypeStruct((nb, t_ctx, D_MODEL), F32), jax.ShapeDtypeStruct((nb, t_lat, D_MODEL), F32)],
        scratch_shapes=[pltpu.VMEM((LANES, LANES), F32)],
        compiler_params=_cparams(("parallel", "parallel", "arbitrary")),
        name="wkv_bwd" if rev else "wkv_fwd",
    )(*ctx_ops, *lat_ops)


def _rwkv_out_kernel(yf_ref, yb_ref, bonus_ref, g_ref, lnw_ref, lnb_ref, e_ref, et_ref, wout_ref, x_ref, gpost_ref,
                     gt_ref, o_ref):
    e = e_ref[...]
    et = et_ref[...]
    y = yf_ref[...] + yb_ref[...]
    mean = _seg_bcast(_seg_sum(y, e) * (1.0 / HEAD_DIM), et)
    d = y - mean
    var = _seg_sum(d * d, e) * (1.0 / HEAD_DIM)
    rstd = _seg_bcast(lax.rsqrt(var + GN_EPS), et)
    yn = d * rstd * lnw_ref[...] + lnb_ref[...]
    t = ((yn + bonus_ref[...]) * g_ref[...].astype(F32)).astype(BF16)
    o = _dot(t, wout_ref[...])
    o_ref[...] = x_ref[...] + gt_ref[0] * _rmsnorm(o, gpost_ref[...])


def _rwkv_out(yf, yb, bonus, g, ln_w, ln_b, e, et, w_out, x, g_post, gt, t_len):
    rows = x.shape[0]
    tm = min(256, t_len)
    tps = t_len // tm
    nbm = gt.shape[0]
    modmap = (lambda i: (i // tps, 0, 0)) if nbm > 1 else (lambda i: (0, 0, 0))
    const2 = lambda i: (0, 0)
    rowmap = lambda i: (i, 0)
    rowspec = pl.BlockSpec((tm, D_MODEL), rowmap)
    vec = pl.BlockSpec((1, D_MODEL), const2)
    return pl.pallas_call(
        _rwkv_out_kernel,
        grid=(rows // tm,),
        in_specs=[rowspec, rowspec, rowspec, rowspec, vec, vec,
                  pl.BlockSpec((D_MODEL, LANES), const2), pl.BlockSpec((LANES, D_MODEL), const2),
                  pl.BlockSpec((D_MODEL, D_MODEL), const2), rowspec, vec, pl.BlockSpec((1, 1, D_MODEL), modmap)],
        out_specs=rowspec,
        out_shape=jax.ShapeDtypeStruct((rows, D_MODEL), F32),
        compiler_params=_cparams(("parallel",)),
        name="rwkv_out",
    )(yf, yb, bonus, g, ln_w, ln_b, e, et, w_out, x, g_post, gt)


def kernel(x, c, ctx, c_ctx, mod_w, mod_b, norm_pre, norm_post, f_w_in, f_w_mix, f_b_mix, f_w_out,
           r_mu, r_w_in, r_w0, r_w1, r_w2, r_a0, r_a1, r_a2, r_v0, r_v1, r_v2,
           r_k_k, r_k_a, r_r_k, r_ln_w, r_ln_b, r_w_out):
    nb, t_lat, d = x.shape
    t_ctx = ctx.shape[1]
    assert d == D_MODEL and t_lat % t_ctx == 0 and t_ctx % CHUNK == 0 and t_lat % GRID_W == 0

    cond_rows = 8
    cond = jnp.zeros((cond_rows, d), F32).at[:nb].set(c).at[nb].set(c_ctx)
    mods = _modulation(cond, mod_w, mod_b)

    e, et = _head_indicator()
    tabs_lat = _dft_tables(t_lat)
    tc = jnp.arange(t_ctx, dtype=jnp.int32)
    ang_c = _angles(tc, tc, t_ctx)
    tabs_ctx = (jnp.cos(ang_c).astype(BF16), jnp.sin(ang_c).astype(BF16))

    xl = x.reshape(nb * t_lat, d)
    xc = ctx.reshape(nb * t_ctx, d)
    v_first = None
    vec = lambda a: a.reshape(1, d)

    for i in range(DEPTH):
        last = i == DEPTH - 1
        kind, j = i % 2, i // 2
        m = mods[i]
        sh_l, sc_l, gt_l = (m[:nb, q * d:(q + 1) * d].reshape(nb, 1, d) for q in range(3))
        sh_c, sc_c, gt_c = (m[nb:nb + 1, q * d:(q + 1) * d].reshape(1, 1, d) for q in range(3))
        g_pre, g_post = vec(norm_pre[i]), vec(norm_post[i])
        if kind == 0:
            w_in = f_w_in[j].astype(BF16)
            w_out = f_w_out[j].astype(BF16)
            wcs = _fold_channel_dft(f_w_mix[j])
            b_mix = vec(f_b_mix[j])
            p, sz = _fourier_in(xl, t_lat, g_pre, sc_l, sh_l, w_in, wcs)
            xl_new = _fourier_out(*tabs_lat, p, sz, b_mix, w_out, xl, g_post, gt_l, t_lat)
            if not last:
                p, sz = _fourier_in(xc, t_ctx, g_pre, sc_c, sh_c, w_in, wcs)
                xc = _fourier_out(*tabs_ctx, p, sz, b_mix, w_out, xc, g_post, gt_c, t_ctx)
            xl = xl_new
        else:
            wts = dict(
                mu=r_mu[j], w4=r_w_in[j].astype(BF16),
                w1=jnp.concatenate([r_w1[j, 0], r_w1[j, 1]], axis=1).astype(BF16), w2=r_w2[j].astype(BF16), w0=r_w0[j],
                a1=jnp.concatenate([r_a1[j, 0], r_a1[j, 1]], axis=1).astype(BF16), a2=r_a2[j].astype(BF16), a0=r_a0[j],
                k_k=vec(r_k_k[j]), k_a=vec(r_k_a[j]), r_k=vec(r_r_k[j]), e=e, et=et)
            if j > 0:
                wts.update(v1=r_v1[j - 1].astype(BF16), v2=r_v2[j - 1].astype(BF16), v0=vec(r_v0[j - 1]))
            fc = _rwkv_features(xc, t_ctx, False, g_pre, sc_c, sh_c, wts, None if j == 0 else v_first[0])
            fl = _rwkv_features(xl, t_lat, True, g_pre, sc_l, sh_l, wts, None if j == 0 else v_first[1])
            if j == 0:
                v_first = (fc[1], fl[1])
            ys = []
            for n, rev in enumerate((False, True)):
                pick = lambda f, t: tuple(a.reshape(nb, t, d) for a in (f[0], f[3 + n], f[1], f[2], f[7 + n], f[5 + n]))
                ys.append(_wkv(pick(fc, t_ctx), pick(fl, t_lat), nb, t_ctx, t_lat, rev))
            w_out = r_w_out[j].astype(BF16)
            ln_w, ln_b = vec(r_ln_w[j]), vec(r_ln_b[j])
            xl_new = _rwkv_out(ys[0][1].reshape(-1, d), ys[1][1].reshape(-1, d), fl[10], fl[9], ln_w, ln_b, e, et,
                               w_out, xl, g_post, gt_l, t_lat)
            if not last:
                xc = _rwkv_out(ys[0][0].reshape(-1, d), ys[1][0].reshape(-1, d), fc[10], fc[9], ln_w, ln_b, e, et,
                               w_out, xc, g_post, gt_c, t_ctx)
            xl = xl_new
    return xl.reshape(nb, t_lat, d)
```

```python
import functools
import math

import numpy as np
import jax
import jax.numpy as jnp
from jax import lax
from jax.experimental import pallas as pl
from jax.experimental.pallas import tpu as pltpu

F32 = jnp.float32
BF16 = jnp.bfloat16
HIGHEST = lax.Precision.HIGHEST

D_MODEL = 1024
DEPTH = 4
GRID_W = 64
HEAD_DIM = 64
N_HEADS = D_MODEL // HEAD_DIM
N_FGROUPS = 8
FGROUP_DIM = D_MODEL // N_FGROUPS
LORA_DECAY = 64
LORA_ICLR = 64
LORA_VRES = 32
RMS_EPS = 1e-6
GN_EPS = 64e-5

LANES = 128
CHUNK = 64
HEADS_PER_TILE = LANES // HEAD_DIM
N_PAIRS = D_MODEL // LANES
VMEM_LIMIT = 56 * 1024 * 1024


def _cparams(sem):
    return pltpu.CompilerParams(dimension_semantics=sem, vmem_limit_bytes=VMEM_LIMIT)


def _dot(a, b):
    return jnp.dot(a, b, preferred_element_type=F32)


def _dot_hi(a, b):
    return jnp.dot(a, b, preferred_element_type=F32, precision=HIGHEST)


def _rmsnorm(x, g):
    return x * lax.rsqrt(jnp.mean(x * x, axis=-1, keepdims=True) + RMS_EPS) * g


def _split_bf16(x):
    hi = x.astype(BF16)
    lo = (x - hi.astype(F32)).astype(BF16)
    return hi, lo


def _seg_sum(x, e):
    hi, lo = _split_bf16(x)
    return _dot(hi, e) + _dot(lo, e)


def _seg_bcast(s, et):
    hi, lo = _split_bf16(s)
    return _dot(hi, et) + _dot(lo, et)


def _head_indicator():
    e = np.zeros((D_MODEL, LANES), np.float32)
    e[np.arange(D_MODEL), np.arange(D_MODEL) // HEAD_DIM] = 1.0
    return jnp.asarray(e, BF16), jnp.asarray(e.T, BF16)


def _mod_kernel(cond_ref, w_ref, b_ref, o_ref):
    cond = cond_ref[...]
    s = cond * jax.nn.sigmoid(cond)
    o_ref[0] = _dot_hi(s, w_ref[0]) + b_ref[0]


def _modulation(cond, mod_w, mod_b):
    rows = cond.shape[0]
    tn = 1536
    return pl.pallas_call(
        _mod_kernel,
        grid=(DEPTH, 3 * D_MODEL // tn),
        in_specs=[
            pl.BlockSpec((rows, D_MODEL), lambda i, j: (0, 0)),
            pl.BlockSpec((1, D_MODEL, tn), lambda i, j: (i, 0, j)),
            pl.BlockSpec((1, 1, tn), lambda i, j: (i, 0, j)),
        ],
        out_specs=pl.BlockSpec((1, rows, tn), lambda i, j: (i, 0, j)),
        out_shape=jax.ShapeDtypeStruct((DEPTH, rows, 3 * D_MODEL), F32),
        compiler_params=_cparams(("parallel", "parallel")),
        name="modulation",
    )(cond, mod_w, mod_b.reshape(DEPTH, 1, 3 * D_MODEL))


def _wcs_kernel(cc_ref, sc_ref, w_ref, o_ref):
    w = w_ref[0]
    o_ref[0, :, :FGROUP_DIM] = _dot_hi(cc_ref[...], w).astype(o_ref.dtype)
    o_ref[0, :, FGROUP_DIM:] = _dot_hi(sc_ref[...], w).astype(o_ref.dtype)


def _fold_channel_dft(w_mix):
    n = np.arange(FGROUP_DIM)
    ang = 2.0 * np.pi * ((n[:, None] * n[None, :]) % FGROUP_DIM) / FGROUP_DIM
    cc = jnp.asarray(np.cos(ang), F32)
    sc = jnp.asarray(np.sin(ang), F32)
    return pl.pallas_call(
        _wcs_kernel,
        grid=(N_FGROUPS,),
        in_specs=[
            pl.BlockSpec((FGROUP_DIM, FGROUP_DIM), lambda g: (0, 0)),
            pl.BlockSpec((FGROUP_DIM, FGROUP_DIM), lambda g: (0, 0)),
            pl.BlockSpec((1, FGROUP_DIM, FGROUP_DIM), lambda g: (g, 0, 0)),
        ],
        out_specs=pl.BlockSpec((1, FGROUP_DIM, 2 * FGROUP_DIM), lambda g: (g, 0, 0)),
        out_shape=jax.ShapeDtypeStruct((N_FGROUPS, FGROUP_DIM, 2 * FGROUP_DIM), BF16),
        compiler_params=_cparams(("parallel",)),
        name="fold_channel_dft",
    )(cc, sc, w_mix)


def _dft_table_kernel(ca_ref, sa_ref, cb_ref, sb_ref, c_ref, s_ref):
    ca = ca_ref[0]
    sa = sa_ref[0]
    cb = cb_ref[...]
    sb = sb_ref[...]
    c_ref[...] = (ca * cb - sa * sb).astype(c_ref.dtype)
    s_ref[...] = (sa * cb + ca * sb).astype(s_ref.dtype)


def _angles(p, t, period):
    return (2.0 * math.pi / period) * ((p[:, None] * t[None, :]) % period).astype(F32)


def _dft_tables(t_len):
    tp = min(256, t_len)
    tt = min(2048, t_len)
    nb = t_len // tp
    t = jnp.arange(t_len, dtype=jnp.int32)
    ang_a = _angles(jnp.arange(nb, dtype=jnp.int32) * tp, t, t_len)
    ang_b = _angles(jnp.arange(tp, dtype=jnp.int32), t, t_len)
    ca = jnp.cos(ang_a).reshape(nb, 1, t_len)
    sa = jnp.sin(ang_a).reshape(nb, 1, t_len)
    cb = jnp.cos(ang_b)
    sb = jnp.sin(ang_b)
    return pl.pallas_call(
        _dft_table_kernel,
        grid=(t_len // tt, nb),
        in_specs=[
            pl.BlockSpec((1, 1, tt), lambda j, i: (i, 0, j)),
            pl.BlockSpec((1, 1, tt), lambda j, i: (i, 0, j)),
            pl.BlockSpec((tp, tt), lambda j, i: (0, j)),
            pl.BlockSpec((tp, tt), lambda j, i: (0, j)),
        ],
        out_specs=[
            pl.BlockSpec((tp, tt), lambda j, i: (i, j)),
            pl.BlockSpec((tp, tt), lambda j, i: (i, j)),
        ],
        out_shape=[jax.ShapeDtypeStruct((t_len, t_len), BF16)] * 2,
        compiler_params=_cparams(("parallel", "parallel")),
        name="dft_tables",
    )(ca, sa, cb, sb)


def _fourier_in_kernel(x_ref, g_ref, sc_ref, sh_ref, win_ref, wcs_ref, p_ref, sz_ref):
    h = _rmsnorm(x_ref[...], g_ref[...]) * (1.0 + sc_ref[0]) + sh_ref[0]
    uz = _dot(h.astype(BF16), win_ref[...])
    z = uz[:, D_MODEL:]
    sz_ref[...] = (z * jax.nn.sigmoid(z)).astype(sz_ref.dtype)
    for g in range(N_FGROUPS):
        lo, hi = g * FGROUP_DIM, (g + 1) * FGROUP_DIM
        pg = _dot(uz[:, lo:hi].astype(BF16), wcs_ref[g])
        p_ref[:, lo:hi] = pg[:, :FGROUP_DIM].astype(p_ref.dtype)
        p_ref[:, D_MODEL + lo:D_MODEL + hi] = pg[:, FGROUP_DIM:].astype(p_ref.dtype)


def _fourier_in(x, t_len, g_pre, sc, sh, w_in, wcs):
    rows = x.shape[0]
    tm = min(512, t_len)
    tpb = t_len // tm
    nbm = sc.shape[0]
    modmap = (lambda i: (i // tpb, 0, 0)) if nbm > 1 else (lambda i: (0, 0, 0))
    return pl.pallas_call(
        _fourier_in_kernel,
        grid=(rows // tm,),
        in_specs=[
            pl.BlockSpec((tm, D_MODEL), lambda i: (i, 0)),
            pl.BlockSpec((1, D_MODEL), lambda i: (0, 0)),
            pl.BlockSpec((1, 1, D_MODEL), modmap),
            pl.BlockSpec((1, 1, D_MODEL), modmap),
            pl.BlockSpec((D_MODEL, 2 * D_MODEL), lambda i: (0, 0)),
            pl.BlockSpec((N_FGROUPS, FGROUP_DIM, 2 * FGROUP_DIM), lambda i: (0, 0, 0)),
        ],
        out_specs=[
            pl.BlockSpec((tm, 2 * D_MODEL), lambda i: (i, 0)),
            pl.BlockSpec((tm, D_MODEL), lambda i: (i, 0)),
        ],
        out_shape=[
            jax.ShapeDtypeStruct((rows, 2 * D_MODEL), BF16),
            jax.ShapeDtypeStruct((rows, D_MODEL), BF16),
        ],
        compiler_params=_cparams(("parallel",)),
        name="fourier_in",
    )(x, g_pre, sc, sh, w_in, wcs)


def _fourier_out_kernel(c_ref, s_ref, pc_ref, ps_ref, sz_ref, bmix_ref, wout_ref, x_ref, gpost_ref, gt_ref,
                        o_ref, acc_ref, *, scale):
    k = pl.program_id(2)

    @pl.when(k == 0)
    def _():
        acc_ref[...] = jnp.zeros_like(acc_ref)

    acc_ref[...] += _dot(c_ref[...], pc_ref[...]) - _dot(s_ref[...], ps_ref[...])

    @pl.when(k == pl.num_programs(2) - 1)
    def _():
        f = acc_ref[...] * scale + bmix_ref[...]
        gated = (f * sz_ref[...].astype(F32)).astype(BF16)
        o = _dot(gated, wout_ref[...])
        o_ref[...] = x_ref[...] + gt_ref[0] * _rmsnorm(o, gpost_ref[...])


def _fourier_out(c_tab, s_tab, p, sz, b_mix, w_out, x, g_post, gt, t_len):
    rows = x.shape[0]
    nb = rows // t_len
    tm = min(512, t_len)
    tk = min(1024, t_len)
    nti, ntk = t_len // tm, t_len // tk
    nbm = gt.shape[0]
    modmap = (lambda b, i, k: (b, 0, 0)) if nbm > 1 else (lambda b, i, k: (0, 0, 0))
    scale = 1.0 / math.sqrt(t_len * FGROUP_DIM)
    return pl.pallas_call(
        functools.partial(_fourier_out_kernel, scale=scale),
        grid=(nb, nti, ntk),
        in_specs=[
            pl.BlockSpec((tm, tk), lambda b, i, k: (i, k)),
            pl.BlockSpec((tm, tk), lambda b, i, k: (i, k)),
            pl.BlockSpec((tk, D_MODEL), lambda b, i, k: (b * ntk + k, 0)),
            pl.BlockSpec((tk, D_MODEL), lambda b, i, k: (b * ntk + k, 1)),
            pl.BlockSpec((tm, D_MODEL), lambda b, i, k: (b * nti + i, 0)),
            pl.BlockSpec((1, D_MODEL), lambda b, i, k: (0, 0)),
            pl.BlockSpec((D_MODEL, D_MODEL), lambda b, i, k: (0, 0)),
            pl.BlockSpec((tm, D_MODEL), lambda b, i, k: (b * nti + i, 0)),
            pl.BlockSpec((1, D_MODEL), lambda b, i, k: (0, 0)),
            pl.BlockSpec((1, 1, D_MODEL), modmap),
        ],
        out_specs=pl.BlockSpec((tm, D_MODEL), lambda b, i, k: (b * nti + i, 0)),
        out_shape=jax.ShapeDtypeStruct((rows, D_MODEL), F32),
        scratch_shapes=[pltpu.VMEM((tm, D_MODEL), F32)],
        compiler_params=_cparams(("parallel", "parallel", "arbitrary")),
        name="fourier_out",
    )(c_tab, s_tab, p, p, sz, b_mix, w_out, x, g_post, gt)


def _rwkv_feat_kernel(*refs, grid_shift, tiles_per_seq, has_vres):
    if grid_shift:
        x_ref, xp_ref, xn_ref = refs[:3]
        refs = refs[3:]
    else:
        x_ref = refs[0]
        refs = refs[1:]
    (g_ref, sc_ref, sh_ref, mu_ref, w4_ref, w1_ref, w2_ref, w0_ref, a1_ref, a2_ref, a0_ref,
     kk_ref, ka_ref, rk_ref, e_ref, et_ref) = refs[:16]
    refs = refs[16:]
    if has_vres:
        vf_ref, v1_ref, v2_ref, v0_ref = refs[:4]
        refs = refs[4:]
    (r_out, v_out, kkn_out, k0_out, k1_out, lw0_out, lw1_out, b0_out, b1_out, g_out, bonus_out) = refs

    g = g_ref[...]
    sc1 = 1.0 + sc_ref[0]
    sh = sh_ref[0]
    tm = x_ref.shape[0]
    h = _rmsnorm(x_ref[...], g) * sc1 + sh
    row = lax.broadcasted_iota(jnp.int32, (tm, 1), 0)
    if grid_shift:
        i = pl.program_id(0)
        first = (i % tiles_per_seq) == 0
        last = (i % tiles_per_seq) == tiles_per_seq - 1
        hp = _rmsnorm(xp_ref[...], g) * sc1 + sh
        hn = _rmsnorm(xn_ref[...], g) * sc1 + sh
        hp = jnp.where(first, 0.0, hp)
        hn = jnp.where(last, 0.0, hn)
        hall = jnp.concatenate([hp, h, hn], axis=0)
        tot = tm + 2 * GRID_W
        up = hall[0:tm]
        down = hall[2 * GRID_W:2 * GRID_W + tm]
        left = pltpu.roll(hall, 1, axis=0)[GRID_W:GRID_W + tm]
        right = pltpu.roll(hall, tot - 1, axis=0)[GRID_W:GRID_W + tm]
        col = row % GRID_W
        left = jnp.where(col == 0, 0.0, left)
        right = jnp.where(col == GRID_W - 1, 0.0, right)
        nbr = (up + down + left + right) * 0.25
    else:
        left = jnp.where(row == 0, 0.0, pltpu.roll(h, 1, axis=0))
        right = jnp.where(row == tm - 1, 0.0, pltpu.roll(h, tm - 1, axis=0))
        nbr = (left + right) * 0.5
    dlt = nbr - h

    def mix(p):
        return h + dlt * mu_ref[p:p + 1, :]

    xr, xk, xv, xg, xw, xa = (mix(p).astype(BF16) for p in range(6))
    r = _dot(xr, w4_ref[0])
    k = _dot(xk, w4_ref[1])
    v = _dot(xv, w4_ref[2])
    gz = _dot(xg, w4_ref[3])
    if has_vres:
        vz = v0_ref[...] + _dot(_dot(xv, v1_ref[...]).astype(BF16), v2_ref[...])
        v = v + (vf_ref[...] - v) * jax.nn.sigmoid(vz)

    e = e_ref[...]
    et = et_ref[...]
    kk = k * kk_ref[...]
    n2 = _seg_sum(kk * kk, e)
    inv = 1.0 / jnp.maximum(jnp.sqrt(n2), 1e-12)
    kkn = kk * _seg_bcast(inv, et)

    tw = jnp.tanh(_dot(xw, w1_ref[...])).astype(BF16)
    ta = _dot(xa, a1_ref[...]).astype(BF16)
    decay_gain = -math.exp(-0.5)
    rsum = jnp.zeros_like(r)
    for n, (lw_out, k_out, b_out) in enumerate(((lw0_out, k0_out, b0_out), (lw1_out, k1_out, b1_out))):
        wz = w0_ref[n:n + 1, :] + _dot(tw[:, n * LORA_DECAY:(n + 1) * LORA_DECAY], w2_ref[n])
        lw_out[...] = decay_gain * jax.nn.sigmoid(wz)
        az = a0_ref[n:n + 1, :] + _dot(ta[:, n * LORA_ICLR:(n + 1) * LORA_ICLR], a2_ref[n])
        a = jax.nn.sigmoid(az)
        kd = k * (1.0 + (a - 1.0) * ka_ref[...])
        k_out[...] = kd
        b_out[...] = kkn * a
        rsum = rsum + r * kd * rk_ref[...]
    bonus_out[...] = _seg_bcast(_seg_sum(rsum, e), et) * v
    r_out[...] = r
    v_out[...] = v
    kkn_out[...] = kkn
    g_out[...] = (gz * jax.nn.sigmoid(gz)).astype(g_out.dtype)


def _rwkv_features(x, t_len, grid_shift, g_pre, sc, sh, wts, v_first):
    rows = x.shape[0]
    tm = 256
    tps = t_len // tm
    assert grid_shift or tps == 1
    nbm = sc.shape[0]
    modmap = (lambda i: (i // tps, 0, 0)) if nbm > 1 else (lambda i: (0, 0, 0))
    const2 = lambda i: (0, 0)
    const3 = lambda i: (0, 0, 0)
    rowmap = lambda i: (i, 0)
    has_vres = v_first is not None
    hb = tm // GRID_W
    nhalo = rows // GRID_W

    args, specs = [x], [pl.BlockSpec((tm, D_MODEL), rowmap)]
    if grid_shift:
        args += [x, x]
        specs += [
            pl.BlockSpec((GRID_W, D_MODEL), lambda i: (jnp.maximum(i * hb - 1, 0), 0)),
            pl.BlockSpec((GRID_W, D_MODEL), lambda i: (jnp.minimum((i + 1) * hb, nhalo - 1), 0)),
        ]
    args += [g_pre, sc, sh, wts["mu"], wts["w4"], wts["w1"], wts["w2"], wts["w0"], wts["a1"], wts["a2"], wts["a0"],
             wts["k_k"], wts["k_a"], wts["r_k"], wts["e"], wts["et"]]
    specs += [
        pl.BlockSpec((1, D_MODEL), const2),
        pl.BlockSpec((1, 1, D_MODEL), modmap),
        pl.BlockSpec((1, 1, D_MODEL), modmap),
        pl.BlockSpec((6, D_MODEL), const2),
        pl.BlockSpec((4, D_MODEL, D_MODEL), const3),
        pl.BlockSpec((D_MODEL, 2 * LORA_DECAY), const2),
        pl.BlockSpec((2, LORA_DECAY, D_MODEL), const3),
        pl.BlockSpec((2, D_MODEL), const2),
        pl.BlockSpec((D_MODEL, 2 * LORA_ICLR), const2),
        pl.BlockSpec((2, LORA_ICLR, D_MODEL), const3),
        pl.BlockSpec((2, D_MODEL), const2),
        pl.BlockSpec((1, D_MODEL), const2),
        pl.BlockSpec((1, D_MODEL), const2),
        pl.BlockSpec((1, D_MODEL), const2),
        pl.BlockSpec((D_MODEL, LANES), const2),
        pl.BlockSpec((LANES, D_MODEL), const2),
    ]
    if has_vres:
        args += [v_first, wts["v1"], wts["v2"], wts["v0"]]
        specs += [
            pl.BlockSpec((tm, D_MODEL), rowmap),
            pl.BlockSpec((D_MODEL, LORA_VRES), const2),
            pl.BlockSpec((LORA_VRES, D_MODEL), const2),
            pl.BlockSpec((1, D_MODEL), const2),
        ]
    n_out = 11
    out_dtypes = [F32] * 9 + [BF16, F32]
    return pl.pallas_call(
        functools.partial(_rwkv_feat_kernel, grid_shift=grid_shift, tiles_per_seq=tps, has_vres=has_vres),
        grid=(rows // tm,),
        in_specs=specs,
        out_specs=[pl.BlockSpec((tm, D_MODEL), rowmap)] * n_out,
        out_shape=[jax.ShapeDtypeStruct((rows, D_MODEL), dt) for dt in out_dtypes],
        compiler_params=_cparams(("parallel",)),
        name="rwkv_features",
    )(*args)


NN = (((1,), (0,)), ((), ()))
NT = (((1,), (1,)), ((), ()))
TN = (((0,), (0,)), ((), ()))
ONE_PASS = (1, 1)
LOG_DECAY_PASSES = (1, 2)
READOUT_PASSES = (1, 2)
STATE_PASSES = (2, 1)


def _parts(x, n):
    out = []
    rem = x
    for i in range(n):
        p = rem.astype(BF16)
        out.append(p)
        if i + 1 < n:
            rem = rem - p.astype(F32)
    return tuple(out)


def _mm(a, b, dn, passes=ONE_PASS):
    na, nb = passes
    ap, bp = _parts(a, na), _parts(b, nb)
    lim = max(na, nb)
    acc = None
    for i, x in enumerate(ap):
        for j, y in enumerate(bp):
            if i + j < lim:
                t = lax.dot_general(x, y, dn, preferred_element_type=F32)
                acc = t if acc is None else acc + t
    return acc


def _wkv_pre_units(units, rev):
    cl = CHUNK
    n2 = HEADS_PER_TILE * cl
    nu = range(len(units))
    ti = lax.broadcasted_iota(jnp.int32, (cl, cl), 0)
    si = lax.broadcasted_iota(jnp.int32, (cl, cl), 1)
    tri = ((si >= ti) if rev else (si <= ti)).astype(BF16)
    first_head = lax.broadcasted_iota(jnp.int32, (cl, LANES), 1) < HEAD_DIM
    rt = lax.broadcasted_iota(jnp.int32, (n2, n2), 0)
    ct = lax.broadcasted_iota(jnp.int32, (n2, n2), 1)
    eye = (rt == ct).astype(F32)
    rt, ct = rt % cl, ct % cl
    strict = (ct > rt) if rev else (ct < rt)
    incl = (ct >= rt) if rev else (ct <= rt)

    def stack(x):
        return jnp.concatenate([jnp.where(first_head, x, 0.0), jnp.where(first_head, 0.0, x)], axis=0).astype(BF16)

    cum = [_mm(tri, u[5], NN, LOG_DECAY_PASSES) for u in units]
    tot = [jnp.sum(u[5], axis=0, keepdims=True) for u in units]
    a_t = [stack(-units[i][3] * jnp.exp(cum[i] - units[i][5])) for i in nu]
    r_t = [stack(units[i][0] * jnp.exp(cum[i])) for i in nu]
    e_neg = [jnp.exp(-cum[i]) for i in nu]
    b_h = [stack(units[i][4] * e_neg[i]) for i in nu]
    k_h = [stack(units[i][1] * e_neg[i]) for i in nu]
    e_rest = [jnp.exp(tot[i] - cum[i]) for i in nu]
    b_r = [stack(units[i][4] * e_rest[i]) for i in nu]
    k_r = [stack(units[i][1] * e_rest[i]) for i in nu]
    v_s = [stack(u[2]) for u in units]

    scores = [_mm(jnp.concatenate([a_t[i], r_t[i]], axis=0), jnp.concatenate([b_h[i], k_h[i]], axis=0), NT) for i in nu]
    a_ab = [jnp.where(strict, sc[:n2, :n2], 0.0) for sc in scores]
    a_ak = [jnp.where(strict, sc[:n2, n2:], 0.0).astype(BF16) for sc in scores]
    a_rb = [jnp.where(incl, sc[n2:, :n2], 0.0).astype(BF16) for sc in scores]
    a_rk = [jnp.where(incl, sc[n2:, n2:], 0.0).astype(BF16) for sc in scores]

    def same_block(s):
        return (rt // s) == (ct // s)

    inv = [eye + jnp.where(same_block(2), x, 0.0) for x in a_ab]
    s = 2
    while s < cl:
        off = same_block(2 * s) & jnp.logical_not(same_block(s))
        xs = [_mm(jnp.where(off, a_ab[i], 0.0), inv[i], NN) for i in nu]
        inv = [inv[i] + _mm(inv[i], xs[i], NN) for i in nu]
        s *= 2

    x2 = [_mm(a_ak[i], v_s[i], NN).astype(BF16) for i in nu]
    tu = [_mm(inv[i], jnp.concatenate([a_t[i], x2[i]], axis=1), NN).astype(BF16) for i in nu]
    m = [eye * jnp.exp(tot[i]) + _mm(tu[i][:, :LANES], b_r[i], TN) for i in nu]
    g = [_mm(jnp.concatenate([tu[i][:, LANES:], v_s[i]], axis=0), jnp.concatenate([b_r[i], k_r[i]], axis=0), TN)
         for i in nu]
    t1 = [_mm(a_rb[i], tu[i], NN) for i in nu]
    qs = [r_t[i].astype(F32) + t1[i][:, :LANES] for i in nu]
    ys = [t1[i][:, LANES:] + _mm(a_rk[i], v_s[i], NN) for i in nu]
    return [(m[i], g[i], qs[i][:cl] + qs[i][cl:], ys[i][:cl] + ys[i][cl:]) for i in nu]


def _wkv_pre_kernel(rc, kc, vc, ac, bc, wc, rl, kl, vl, al, bl, wl, m_ref, g_ref, q_ref, y0_ref, *, rev, cpb, ppb):
    j = pl.program_id(2)

    def run(src):
        ids = [(p, c) for p in range(ppb) for c in range(cpb)]
        units = [tuple(ref[0, pl.ds(c * CHUNK, CHUNK), pl.ds(p * LANES, LANES)] for ref in src) for p, c in ids]
        for (p, c), (m, g, q, y0) in zip(ids, _wkv_pre_units(units, rev)):
            m_ref[0, p, c] = m.astype(m_ref.dtype)
            g_ref[0, p, c] = g.astype(g_ref.dtype)
            q_ref[0, p, c] = q.astype(q_ref.dtype)
            y0_ref[0, p, c] = y0.astype(y0_ref.dtype)

    @pl.when(j == 0)
    def _():
        run((rc, kc, vc, ac, bc, wc))

    @pl.when(j > 0)
    def _():
        run((rl, kl, vl, al, bl, wl))


WKV_PAIRS_PER_BLOCK = 2


def _wkv_pre(ctx_ops, lat_ops, nb, t_ctx, t_lat, rev):
    bt = t_ctx
    cpb = bt // CHUNK
    ppb = WKV_PAIRS_PER_BLOCK
    nlb = t_lat // bt
    nch = (t_ctx + t_lat) // CHUNK
    latmap = lambda b, p, j: (b, jnp.maximum(j - 1, 0), p)
    ctxmap = lambda b, p, j: (b, 0, p)
    outmap = lambda b, p, j: (b, p, j, 0, 0)
    blk = (1, bt, ppb * LANES)
    return pl.pallas_call(
        functools.partial(_wkv_pre_kernel, rev=rev, cpb=cpb, ppb=ppb),
        grid=(nb, N_PAIRS // ppb, nlb + 1),
        in_specs=[pl.BlockSpec(blk, ctxmap)] * 6 + [pl.BlockSpec(blk, latmap)] * 6,
        out_specs=[pl.BlockSpec((1, ppb, cpb, LANES, LANES), outmap)] * 2
        + [pl.BlockSpec((1, ppb, cpb, CHUNK, LANES), outmap)] * 2,
        out_shape=[jax.ShapeDtypeStruct((nb, N_PAIRS, nch, LANES, LANES), BF16)] * 2
        + [jax.ShapeDtypeStruct((nb, N_PAIRS, nch, CHUNK, LANES), BF16)] * 2,
        compiler_params=_cparams(("parallel", "parallel", "parallel")),
        name="wkv_pre_bwd" if rev else "wkv_pre_fwd",
    )(*ctx_ops, *lat_ops)


def _wkv_scan_kernel(m_ref, g_ref, q_ref, y0_ref, yc_ref, yl_ref, s_ref, *, rev, cpb, nb):
    j = pl.program_id(0)

    @pl.when(j == 0)
    def _():
        s_ref[...] = jnp.zeros_like(s_ref)

    def run(y_ref):
        order = range(cpb - 1, -1, -1) if rev else range(cpb)
        for c in order:
            for b in range(nb):
                for p in range(N_PAIRS):
                    n = b * N_PAIRS + p
                    s = s_ref[n]
                    y = _mm(q_ref[b, p, c], s, NT, READOUT_PASSES) + y0_ref[b, p, c].astype(F32)
                    y_ref[b, c * CHUNK:(c + 1) * CHUNK, p * LANES:(p + 1) * LANES] = y
                    s_ref[n] = _mm(s, m_ref[b, p, c], NN, STATE_PASSES) + g_ref[b, p, c].astype(F32)

    @pl.when(j == 0)
    def _():
        run(yc_ref)

    @pl.when(j > 0)
    def _():
        run(yl_ref)


def _wkv_scan(m, g, q, y0, nb, t_ctx, t_lat, rev):
    bt = t_ctx
    cpb = bt // CHUNK
    nlb = t_lat // bt
    if rev:
        blkmap = lambda j: (0, 0, jnp.where(j == 0, 0, nlb + 1 - j), 0, 0)
        latmap = lambda j: (0, jnp.where(j == 0, nlb - 1, nlb - j), 0)
    else:
        blkmap = lambda j: (0, 0, j, 0, 0)
        latmap = lambda j: (0, jnp.maximum(j - 1, 0), 0)
    return pl.pallas_call(
        functools.partial(_wkv_scan_kernel, rev=rev, cpb=cpb, nb=nb),
        grid=(nlb + 1,),
        in_specs=[pl.BlockSpec((nb, N_PAIRS, cpb, LANES, LANES), blkmap)] * 2
        + [pl.BlockSpec((nb, N_PAIRS, cpb, CHUNK, LANES), blkmap)] * 2,
        out_specs=[pl.BlockSpec((nb, bt, D_MODEL), lambda j: (0, 0, 0)), pl.BlockSpec((nb, bt, D_MODEL), latmap)],
        out_shape=[jax.ShapeDtypeStruct((nb, t_ctx, D_MODEL), F32), jax.ShapeDtypeStruct((nb, t_lat, D_MODEL), F32)],
        scratch_shapes=[pltpu.VMEM((nb * N_PAIRS, LANES, LANES), F32)],
        compiler_params=_cparams(("arbitrary",)),
        name="wkv_scan_bwd" if rev else "wkv_scan_fwd",
    )(m, g, q, y0)


def _wkv(ctx_ops, lat_ops, nb, t_ctx, t_lat, rev):
    pre = _wkv_pre(ctx_ops, lat_ops, nb, t_ctx, t_lat, rev)
    return _wkv_scan(*pre, nb, t_ctx, t_lat, rev)


def _rwkv_out_kernel(yf_ref, yb_ref, bonus_ref, g_ref, lnw_ref, lnb_ref, e_ref, et_ref, wout_ref, x_ref, gpost_ref,
                     gt_ref, o_ref):
    e = e_ref[...]
    et = et_ref[...]
    y = yf_ref[...] + yb_ref[...]
    mean = _seg_bcast(_seg_sum(y, e) * (1.0 / HEAD_DIM), et)
    d = y - mean
    var = _seg_sum(d * d, e) * (1.0 / HEAD_DIM)
    rstd = _seg_bcast(lax.rsqrt(var + GN_EPS), et)
    yn = d * rstd * lnw_ref[...] + lnb_ref[...]
    t = ((yn + bonus_ref[...]) * g_ref[...].astype(F32)).astype(BF16)
    o = _dot(t, wout_ref[...])
    o_ref[...] = x_ref[...] + gt_ref[0] * _rmsnorm(o, gpost_ref[...])


def _rwkv_out(yf, yb, bonus, g, ln_w, ln_b, e, et, w_out, x, g_post, gt, t_len):
    rows = x.shape[0]
    tm = min(256, t_len)
    tps = t_len // tm
    nbm = gt.shape[0]
    modmap = (lambda i: (i // tps, 0, 0)) if nbm > 1 else (lambda i: (0, 0, 0))
    const2 = lambda i: (0, 0)
    rowmap = lambda i: (i, 0)
    rowspec = pl.BlockSpec((tm, D_MODEL), rowmap)
    vec = pl.BlockSpec((1, D_MODEL), const2)
    return pl.pallas_call(
        _rwkv_out_kernel,
        grid=(rows // tm,),
        in_specs=[rowspec, rowspec, rowspec, rowspec, vec, vec,
                  pl.BlockSpec((D_MODEL, LANES), const2), pl.BlockSpec((LANES, D_MODEL), const2),
                  pl.BlockSpec((D_MODEL, D_MODEL), const2), rowspec, vec, pl.BlockSpec((1, 1, D_MODEL), modmap)],
        out_specs=rowspec,
        out_shape=jax.ShapeDtypeStruct((rows, D_MODEL), F32),
        compiler_params=_cparams(("parallel",)),
        name="rwkv_out",
    )(yf, yb, bonus, g, ln_w, ln_b, e, et, w_out, x, g_post, gt)


def kernel(x, c, ctx, c_ctx, mod_w, mod_b, norm_pre, norm_post, f_w_in, f_w_mix, f_b_mix, f_w_out,
           r_mu, r_w_in, r_w0, r_w1, r_w2, r_a0, r_a1, r_a2, r_v0, r_v1, r_v2,
           r_k_k, r_k_a, r_r_k, r_ln_w, r_ln_b, r_w_out):
    nb, t_lat, d = x.shape
    t_ctx = ctx.shape[1]
    assert d == D_MODEL and t_lat % t_ctx == 0 and t_ctx % CHUNK == 0 and t_lat % GRID_W == 0

    cond_rows = 8
    cond = jnp.zeros((cond_rows, d), F32).at[:nb].set(c).at[nb].set(c_ctx)
    mods = _modulation(cond, mod_w, mod_b)

    e, et = _head_indicator()
    tabs_lat = _dft_tables(t_lat)
    tc = jnp.arange(t_ctx, dtype=jnp.int32)
    ang_c = _angles(tc, tc, t_ctx)
    tabs_ctx = (jnp.cos(ang_c).astype(BF16), jnp.sin(ang_c).astype(BF16))

    xl = x.reshape(nb * t_lat, d)
    xc = ctx.reshape(nb * t_ctx, d)
    v_first = None
    vec = lambda a: a.reshape(1, d)

    for i in range(DEPTH):
        last = i == DEPTH - 1
        kind, j = i % 2, i // 2
        m = mods[i]
        sh_l, sc_l, gt_l = (m[:nb, q * d:(q + 1) * d].reshape(nb, 1, d) for q in range(3))
        sh_c, sc_c, gt_c = (m[nb:nb + 1, q * d:(q + 1) * d].reshape(1, 1, d) for q in range(3))
        g_pre, g_post = vec(norm_pre[i]), vec(norm_post[i])
        if kind == 0:
            w_in = f_w_in[j].astype(BF16)
            w_out = f_w_out[j].astype(BF16)
            wcs = _fold_channel_dft(f_w_mix[j])
            b_mix = vec(f_b_mix[j])
            p, sz = _fourier_in(xl, t_lat, g_pre, sc_l, sh_l, w_in, wcs)
            xl_new = _fourier_out(*tabs_lat, p, sz, b_mix, w_out, xl, g_post, gt_l, t_lat)
            if not last:
                p, sz = _fourier_in(xc, t_ctx, g_pre, sc_c, sh_c, w_in, wcs)
                xc = _fourier_out(*tabs_ctx, p, sz, b_mix, w_out, xc, g_post, gt_c, t_ctx)
            xl = xl_new
        else:
            wts = dict(
                mu=r_mu[j], w4=r_w_in[j].astype(BF16),
                w1=jnp.concatenate([r_w1[j, 0], r_w1[j, 1]], axis=1).astype(BF16), w2=r_w2[j].astype(BF16), w0=r_w0[j],
                a1=jnp.concatenate([r_a1[j, 0], r_a1[j, 1]], axis=1).astype(BF16), a2=r_a2[j].astype(BF16), a0=r_a0[j],
                k_k=vec(r_k_k[j]), k_a=vec(r_k_a[j]), r_k=vec(r_r_k[j]), e=e, et=et)
            if j > 0:
                wts.update(v1=r_v1[j - 1].astype(BF16), v2=r_v2[j - 1].astype(BF16), v0=vec(r_v0[j - 1]))
            fc = _rwkv_features(xc, t_ctx, False, g_pre, sc_c, sh_c, wts, None if j == 0 else v_first[0])
            fl = _rwkv_features(xl, t_lat, True, g_pre, sc_l, sh_l, wts, None if j == 0 else v_first[1])
            if j == 0:
                v_first = (fc[1], fl[1])
            ys = []
            for n, rev in enumerate((False, True)):
                pick = lambda f, t: tuple(a.reshape(nb, t, d) for a in (f[0], f[3 + n], f[1], f[2], f[7 + n], f[5 + n]))
                ys.append(_wkv(pick(fc, t_ctx), pick(fl, t_lat), nb, t_ctx, t_lat, rev))
            w_out = r_w_out[j].astype(BF16)
            ln_w, ln_b = vec(r_ln_w[j]), vec(r_ln_b[j])
            xl_new = _rwkv_out(ys[0][1].reshape(-1, d), ys[1][1].reshape(-1, d), fl[10], fl[9], ln_w, ln_b, e, et,
                               w_out, xl, g_post, gt_l, t_lat)
            if not last:
                xc = _rwkv_out(ys[0][0].reshape(-1, d), ys[1][0].reshape(-1, d), fc[10], fc[9], ln_w, ln_b, e, et,
                               w_out, xc, g_post, gt_c, t_ctx)
            xl = xl_new
    return xl.reshape(nb, t_lat, d)
```

```python
import functools
import math

import numpy as np
import jax
import jax.numpy as jnp
from jax import lax
from jax.experimental import pallas as pl
from jax.experimental.pallas import tpu as pltpu

F32 = jnp.float32
BF16 = jnp.bfloat16
HIGHEST = lax.Precision.HIGHEST

D_MODEL = 1024
DEPTH = 4
GRID_W = 64
HEAD_DIM = 64
N_HEADS = D_MODEL // HEAD_DIM
N_FGROUPS = 8
FGROUP_DIM = D_MODEL // N_FGROUPS
LORA_DECAY = 64
LORA_ICLR = 64
LORA_VRES = 32
RMS_EPS = 1e-6
GN_EPS = 64e-5

LANES = 128
CHUNK = 64
HEADS_PER_TILE = LANES // HEAD_DIM
N_PAIRS = D_MODEL // LANES
VMEM_LIMIT = 56 * 1024 * 1024


def _cparams(sem):
    return pltpu.CompilerParams(dimension_semantics=sem, vmem_limit_bytes=VMEM_LIMIT)


def _dot(a, b):
    return jnp.dot(a, b, preferred_element_type=F32)


def _dot_hi(a, b):
    return jnp.dot(a, b, preferred_element_type=F32, precision=HIGHEST)


def _rmsnorm(x, g):
    return x * lax.rsqrt(jnp.mean(x * x, axis=-1, keepdims=True) + RMS_EPS) * g


def _split_bf16(x):
    hi = x.astype(BF16)
    lo = (x - hi.astype(F32)).astype(BF16)
    return hi, lo


def _seg_sum(x, e):
    hi, lo = _split_bf16(x)
    return _dot(hi, e) + _dot(lo, e)


def _seg_bcast(s, et):
    hi, lo = _split_bf16(s)
    return _dot(hi, et) + _dot(lo, et)


def _head_indicator():
    e = np.zeros((D_MODEL, LANES), np.float32)
    e[np.arange(D_MODEL), np.arange(D_MODEL) // HEAD_DIM] = 1.0
    return jnp.asarray(e, BF16), jnp.asarray(e.T, BF16)


def _mod_kernel(cond_ref, w_ref, b_ref, o_ref):
    cond = cond_ref[...]
    s = cond * jax.nn.sigmoid(cond)
    o_ref[0] = _dot_hi(s, w_ref[0]) + b_ref[0]


def _modulation(cond, mod_w, mod_b):
    rows = cond.shape[0]
    tn = 1536
    return pl.pallas_call(
        _mod_kernel,
        grid=(DEPTH, 3 * D_MODEL // tn),
        in_specs=[
            pl.BlockSpec((rows, D_MODEL), lambda i, j: (0, 0)),
            pl.BlockSpec((1, D_MODEL, tn), lambda i, j: (i, 0, j)),
            pl.BlockSpec((1, 1, tn), lambda i, j: (i, 0, j)),
        ],
        out_specs=pl.BlockSpec((1, rows, tn), lambda i, j: (i, 0, j)),
        out_shape=jax.ShapeDtypeStruct((DEPTH, rows, 3 * D_MODEL), F32),
        compiler_params=_cparams(("parallel", "parallel")),
        name="modulation",
    )(cond, mod_w, mod_b.reshape(DEPTH, 1, 3 * D_MODEL))


def _wcs_kernel(cc_ref, sc_ref, w_ref, o_ref):
    w = w_ref[0]
    o_ref[0, :, :FGROUP_DIM] = _dot_hi(cc_ref[...], w).astype(o_ref.dtype)
    o_ref[0, :, FGROUP_DIM:] = _dot_hi(sc_ref[...], w).astype(o_ref.dtype)


def _fold_channel_dft(w_mix):
    n = np.arange(FGROUP_DIM)
    ang = 2.0 * np.pi * ((n[:, None] * n[None, :]) % FGROUP_DIM) / FGROUP_DIM
    cc = jnp.asarray(np.cos(ang), F32)
    sc = jnp.asarray(np.sin(ang), F32)
    return pl.pallas_call(
        _wcs_kernel,
        grid=(N_FGROUPS,),
        in_specs=[
            pl.BlockSpec((FGROUP_DIM, FGROUP_DIM), lambda g: (0, 0)),
            pl.BlockSpec((FGROUP_DIM, FGROUP_DIM), lambda g: (0, 0)),
            pl.BlockSpec((1, FGROUP_DIM, FGROUP_DIM), lambda g: (g, 0, 0)),
        ],
        out_specs=pl.BlockSpec((1, FGROUP_DIM, 2 * FGROUP_DIM), lambda g: (g, 0, 0)),
        out_shape=jax.ShapeDtypeStruct((N_FGROUPS, FGROUP_DIM, 2 * FGROUP_DIM), BF16),
        compiler_params=_cparams(("parallel",)),
        name="fold_channel_dft",
    )(cc, sc, w_mix)


DFT_TILE = 512
DFT_TILE_PAD = 16


def _dft_tile(t_len):
    return min(DFT_TILE, t_len // 2)


def _dft_table_kernel(ca_ref, sa_ref, cb_ref, sb_ref, c_ref, s_ref):
    ca = ca_ref[0]
    sa = sa_ref[0]
    cb = cb_ref[...]
    sb = sb_ref[...]
    c_ref[0] = (ca * cb - sa * sb).astype(c_ref.dtype)
    s_ref[0] = (sa * cb + ca * sb).astype(s_ref.dtype)


def _angles(p, t, period):
    return (2.0 * math.pi / period) * ((p[:, None] * t[None, :]) % period).astype(F32)


def _dft_tables(t_len):
    half = t_len // 2
    tm = _dft_tile(t_len)
    rows = tm + DFT_TILE_PAD
    nti = half // tm
    tt = min(2048, half)
    t = jnp.arange(half, dtype=jnp.int32)
    ang_a = _angles(jnp.arange(nti, dtype=jnp.int32) * tm, t, t_len)
    ang_b = _angles(jnp.arange(rows, dtype=jnp.int32), t, t_len)
    ca = jnp.cos(ang_a).reshape(nti, 1, half)
    sa = jnp.sin(ang_a).reshape(nti, 1, half)
    cb = jnp.cos(ang_b)
    sb = jnp.sin(ang_b)
    return pl.pallas_call(
        _dft_table_kernel,
        grid=(half // tt, nti),
        in_specs=[
            pl.BlockSpec((1, 1, tt), lambda j, i: (i, 0, j)),
            pl.BlockSpec((1, 1, tt), lambda j, i: (i, 0, j)),
            pl.BlockSpec((rows, tt), lambda j, i: (0, j)),
            pl.BlockSpec((rows, tt), lambda j, i: (0, j)),
        ],
        out_specs=[
            pl.BlockSpec((1, rows, tt), lambda j, i: (i, 0, j)),
            pl.BlockSpec((1, rows, tt), lambda j, i: (i, 0, j)),
        ],
        out_shape=[jax.ShapeDtypeStruct((nti, rows, half), BF16)] * 2,
        compiler_params=_cparams(("parallel", "parallel")),
        name="dft_tables",
    )(ca, sa, cb, sb)


def _mirror_select(n_out, n_in, offset):
    i = lax.broadcasted_iota(jnp.int32, (n_out, n_in), 0)
    c = lax.broadcasted_iota(jnp.int32, (n_out, n_in), 1)
    return (c == offset - i).astype(BF16)


def _dft_fold_kernel(a_ref, m_ref, x_ref, o_ref):
    tf = a_ref.shape[0]
    a = a_ref[...].astype(F32)
    src = jnp.concatenate([m_ref[...], x_ref[...]], axis=0)
    mir = _dot(_mirror_select(tf, tf + DFT_TILE_PAD, tf), src)
    t_is_zero = (lax.broadcasted_iota(jnp.int32, (tf, 1), 0) == 0) & (pl.program_id(1) == 0)
    o_ref[:, :D_MODEL] = jnp.where(t_is_zero, a[:, :D_MODEL], a[:, :D_MODEL] + mir[:, :D_MODEL]).astype(o_ref.dtype)
    o_ref[:, D_MODEL:] = (a[:, D_MODEL:] - mir[:, D_MODEL:]).astype(o_ref.dtype)


def _dft_fold(p, t_len):
    rows = p.shape[0]
    nb = rows // t_len
    half = t_len // 2
    tf = _dft_tile(t_len)
    ntf = t_len // tf
    pad = DFT_TILE_PAD
    width = p.shape[1]
    return pl.pallas_call(
        _dft_fold_kernel,
        grid=(nb, half // tf),
        in_specs=[
            pl.BlockSpec((tf, width), lambda b, j: (b * ntf + j, 0)),
            pl.BlockSpec((tf, width), lambda b, j: (b * ntf + ntf - 1 - j, 0)),
            pl.BlockSpec((pad, width), lambda b, j: (b * (t_len // pad) + jnp.where(j == 0, 0, (t_len - j * tf) // pad), 0)),
        ],
        out_specs=pl.BlockSpec((tf, width), lambda b, j: (b * (half // tf) + j, 0)),
        out_shape=jax.ShapeDtypeStruct((nb * half, width), BF16),
        compiler_params=_cparams(("parallel", "parallel")),
        name="dft_fold",
    )(p, p, p)


def _fourier_in_kernel(x_ref, g_ref, sc_ref, sh_ref, win_ref, wcs_ref, p_ref, sz_ref):
    h = _rmsnorm(x_ref[...], g_ref[...]) * (1.0 + sc_ref[0]) + sh_ref[0]
    uz = _dot(h.astype(BF16), win_ref[...])
    z = uz[:, D_MODEL:]
    sz_ref[...] = (z * jax.nn.sigmoid(z)).astype(sz_ref.dtype)
    for g in range(N_FGROUPS):
        lo, hi = g * FGROUP_DIM, (g + 1) * FGROUP_DIM
        pg = _dot(uz[:, lo:hi].astype(BF16), wcs_ref[g])
        p_ref[:, lo:hi] = pg[:, :FGROUP_DIM].astype(p_ref.dtype)
        p_ref[:, D_MODEL + lo:D_MODEL + hi] = pg[:, FGROUP_DIM:].astype(p_ref.dtype)


def _fourier_in(x, t_len, g_pre, sc, sh, w_in, wcs):
    rows = x.shape[0]
    tm = min(512, t_len)
    tpb = t_len // tm
    nbm = sc.shape[0]
    modmap = (lambda i: (i // tpb, 0, 0)) if nbm > 1 else (lambda i: (0, 0, 0))
    return pl.pallas_call(
        _fourier_in_kernel,
        grid=(rows // tm,),
        in_specs=[
            pl.BlockSpec((tm, D_MODEL), lambda i: (i, 0)),
            pl.BlockSpec((1, D_MODEL), lambda i: (0, 0)),
            pl.BlockSpec((1, 1, D_MODEL), modmap),
            pl.BlockSpec((1, 1, D_MODEL), modmap),
            pl.BlockSpec((D_MODEL, 2 * D_MODEL), lambda i: (0, 0)),
            pl.BlockSpec((N_FGROUPS, FGROUP_DIM, 2 * FGROUP_DIM), lambda i: (0, 0, 0)),
        ],
        out_specs=[
            pl.BlockSpec((tm, 2 * D_MODEL), lambda i: (i, 0)),
            pl.BlockSpec((tm, D_MODEL), lambda i: (i, 0)),
        ],
        out_shape=[
            jax.ShapeDtypeStruct((rows, 2 * D_MODEL), BF16),
            jax.ShapeDtypeStruct((rows, D_MODEL), BF16),
        ],
        compiler_params=_cparams(("parallel",)),
        name="fourier_in",
    )(x, g_pre, sc, sh, w_in, wcs)


def _fourier_out_kernel(c_ref, s_ref, ec_ref, os_ref, ph_ref, szd_ref, szm_ref, bmix_ref, wout_ref, xd_ref, xm_ref,
                        gpost_ref, gt_ref, od_ref, om_ref, accc_ref, accs_ref, *, scale):
    i, k = pl.program_id(1), pl.program_id(2)

    @pl.when(k == 0)
    def _():
        accc_ref[...] = jnp.zeros_like(accc_ref)
        accs_ref[...] = jnp.zeros_like(accs_ref)

    accc_ref[...] += _dot(c_ref[0], ec_ref[...])
    accs_ref[...] += _dot(s_ref[0], os_ref[...])

    def finish(f, sz_ref, x_ref, o_ref):
        gated = ((f * scale + bmix_ref[...]) * sz_ref[...].astype(F32)).astype(BF16)
        o = _dot(gated, wout_ref[...])
        o_ref[...] = x_ref[...] + gt_ref[0] * _rmsnorm(o, gpost_ref[...])

    @pl.when(k == pl.num_programs(2) - 1)
    def _():
        rows = accc_ref.shape[0]
        tm = rows - DFT_TILE_PAD
        p = i * tm + lax.broadcasted_iota(jnp.int32, (rows, 1), 0)
        sign = (1 - 2 * (p & 1)).astype(F32)
        fc = accc_ref[...] + sign * ph_ref[0:1, :].astype(F32)
        fs = accs_ref[...]
        finish((fc - fs)[:tm], szd_ref, xd_ref, od_ref)
        hi, lo = _split_bf16(fc + fs)
        sel = _mirror_select(tm, rows, tm)
        finish(_dot(sel, hi) + _dot(sel, lo), szm_ref, xm_ref, om_ref)


def _fourier_out(c_tab, s_tab, eo, p, sz, b_mix, w_out, x, g_post, gt, t_len):
    rows = x.shape[0]
    nb = rows // t_len
    half = t_len // 2
    tm = _dft_tile(t_len)
    tk = min(512, half)
    pad = DFT_TILE_PAD
    nti, ntk, ntf = half // tm, half // tk, t_len // tm
    nbm = gt.shape[0]
    modmap = (lambda b, i, k: (b, 0, 0)) if nbm > 1 else (lambda b, i, k: (0, 0, 0))
    direct = lambda b, i, k: (b * ntf + i, 0)
    mirror = lambda b, i, k: (b * ntf + ntf - 1 - i, 0)
    const = lambda b, i, k: (0, 0)
    rowspec = lambda m: pl.BlockSpec((tm, D_MODEL), m)
    scale = 1.0 / math.sqrt(t_len * FGROUP_DIM)
    return pl.pallas_call(
        functools.partial(_fourier_out_kernel, scale=scale),
        grid=(nb, nti, ntk),
        in_specs=[
            pl.BlockSpec((1, tm + pad, tk), lambda b, i, k: (i, 0, k)),
            pl.BlockSpec((1, tm + pad, tk), lambda b, i, k: (i, 0, k)),
            pl.BlockSpec((tk, D_MODEL), lambda b, i, k: (b * ntk + k, 0)),
            pl.BlockSpec((tk, D_MODEL), lambda b, i, k: (b * ntk + k, 1)),
            pl.BlockSpec((pad, D_MODEL), lambda b, i, k: ((b * t_len + half) // pad, 0)),
            rowspec(direct), rowspec(mirror),
            pl.BlockSpec((1, D_MODEL), const),
            pl.BlockSpec((D_MODEL, D_MODEL), const),
            rowspec(direct), rowspec(mirror),
            pl.BlockSpec((1, D_MODEL), const),
            pl.BlockSpec((1, 1, D_MODEL), modmap),
        ],
        out_specs=[pl.BlockSpec((tm, D_MODEL), lambda b, i, k: (b * nti + i, 0)),
                   pl.BlockSpec((tm, D_MODEL), lambda b, i, k: (b * nti + nti - 1 - i, 0))],
        out_shape=[jax.ShapeDtypeStruct((nb * half, D_MODEL), F32)] * 2,
        scratch_shapes=[pltpu.VMEM((tm + pad, D_MODEL), F32)] * 2,
        compiler_params=_cparams(("parallel", "parallel", "arbitrary")),
        name="fourier_out",
    )(c_tab, s_tab, eo, eo, p, sz, sz, b_mix, w_out, x, x, g_post, gt)


def _rwkv_feat_kernel(*refs, grid_shift, tiles_per_seq, has_vres):
    if grid_shift:
        x_ref, xp_ref, xn_ref = refs[:3]
        refs = refs[3:]
    else:
        x_ref = refs[0]
        refs = refs[1:]
    (g_ref, sc_ref, sh_ref, mu_ref, w4_ref, w1_ref, w2_ref, w0_ref, a1_ref, a2_ref, a0_ref,
     kk_ref, ka_ref, rk_ref, e_ref, et_ref) = refs[:16]
    refs = refs[16:]
    if has_vres:
        vf_ref, v1_ref, v2_ref, v0_ref = refs[:4]
        refs = refs[4:]
    (r_out, v_out, kkn_out, k0_out, k1_out, lw0_out, lw1_out, b0_out, b1_out, g_out, bonus_out) = refs

    g = g_ref[...]
    sc1 = 1.0 + sc_ref[0]
    sh = sh_ref[0]
    tm = x_ref.shape[0]
    h = _rmsnorm(x_ref[...], g) * sc1 + sh
    row = lax.broadcasted_iota(jnp.int32, (tm, 1), 0)
    if grid_shift:
        i = pl.program_id(0)
        first = (i % tiles_per_seq) == 0
        last = (i % tiles_per_seq) == tiles_per_seq - 1
        hp = _rmsnorm(xp_ref[...], g) * sc1 + sh
        hn = _rmsnorm(xn_ref[...], g) * sc1 + sh
        hp = jnp.where(first, 0.0, hp)
        hn = jnp.where(last, 0.0, hn)
        hall = jnp.concatenate([hp, h, hn], axis=0)
        tot = tm + 2 * GRID_W
        up = hall[0:tm]
        down = hall[2 * GRID_W:2 * GRID_W + tm]
        left = pltpu.roll(hall, 1, axis=0)[GRID_W:GRID_W + tm]
        right = pltpu.roll(hall, tot - 1, axis=0)[GRID_W:GRID_W + tm]
        col = row % GRID_W
        left = jnp.where(col == 0, 0.0, left)
        right = jnp.where(col == GRID_W - 1, 0.0, right)
        nbr = (up + down + left + right) * 0.25
    else:
        left = jnp.where(row == 0, 0.0, pltpu.roll(h, 1, axis=0))
        right = jnp.where(row == tm - 1, 0.0, pltpu.roll(h, tm - 1, axis=0))
        nbr = (left + right) * 0.5
    dlt = nbr - h

    def mix(p):
        return h + dlt * mu_ref[p:p + 1, :]

    xr, xk, xv, xg, xw, xa = (mix(p).astype(BF16) for p in range(6))
    r = _dot(xr, w4_ref[0])
    k = _dot(xk, w4_ref[1])
    v = _dot(xv, w4_ref[2])
    gz = _dot(xg, w4_ref[3])
    if has_vres:
        vz = v0_ref[...] + _dot(_dot(xv, v1_ref[...]).astype(BF16), v2_ref[...])
        v = v + (vf_ref[...] - v) * jax.nn.sigmoid(vz)

    e = e_ref[...]
    et = et_ref[...]
    kk = k * kk_ref[...]
    n2 = _seg_sum(kk * kk, e)
    inv = 1.0 / jnp.maximum(jnp.sqrt(n2), 1e-12)
    kkn = kk * _seg_bcast(inv, et)

    tw = jnp.tanh(_dot(xw, w1_ref[...])).astype(BF16)
    ta = _dot(xa, a1_ref[...]).astype(BF16)
    decay_gain = -math.exp(-0.5)
    rsum = jnp.zeros_like(r)
    for n, (lw_out, k_out, b_out) in enumerate(((lw0_out, k0_out, b0_out), (lw1_out, k1_out, b1_out))):
        wz = w0_ref[n:n + 1, :] + _dot(tw[:, n * LORA_DECAY:(n + 1) * LORA_DECAY], w2_ref[n])
        lw_out[...] = decay_gain * jax.nn.sigmoid(wz)
        az = a0_ref[n:n + 1, :] + _dot(ta[:, n * LORA_ICLR:(n + 1) * LORA_ICLR], a2_ref[n])
        a = jax.nn.sigmoid(az)
        kd = k * (1.0 + (a - 1.0) * ka_ref[...])
        k_out[...] = kd
        b_out[...] = kkn * a
        rsum = rsum + r * kd * rk_ref[...]
    bonus_out[...] = _seg_bcast(_seg_sum(rsum, e), et) * v
    r_out[...] = r
    v_out[...] = v
    kkn_out[...] = kkn
    g_out[...] = (gz * jax.nn.sigmoid(gz)).astype(g_out.dtype)


def _rwkv_features(x, t_len, grid_shift, g_pre, sc, sh, wts, v_first):
    rows = x.shape[0]
    tm = 256
    tps = t_len // tm
    assert grid_shift or tps == 1
    nbm = sc.shape[0]
    modmap = (lambda i: (i // tps, 0, 0)) if nbm > 1 else (lambda i: (0, 0, 0))
    const2 = lambda i: (0, 0)
    const3 = lambda i: (0, 0, 0)
    rowmap = lambda i: (i, 0)
    has_vres = v_first is not None
    hb = tm // GRID_W
    nhalo = rows // GRID_W

    args, specs = [x], [pl.BlockSpec((tm, D_MODEL), rowmap)]
    if grid_shift:
        args += [x, x]
        specs += [
            pl.BlockSpec((GRID_W, D_MODEL), lambda i: (jnp.maximum(i * hb - 1, 0), 0)),
            pl.BlockSpec((GRID_W, D_MODEL), lambda i: (jnp.minimum((i + 1) * hb, nhalo - 1), 0)),
        ]
    args += [g_pre, sc, sh, wts["mu"], wts["w4"], wts["w1"], wts["w2"], wts["w0"], wts["a1"], wts["a2"], wts["a0"],
             wts["k_k"], wts["k_a"], wts["r_k"], wts["e"], wts["et"]]
    specs += [
        pl.BlockSpec((1, D_MODEL), const2),
        pl.BlockSpec((1, 1, D_MODEL), modmap),
        pl.BlockSpec((1, 1, D_MODEL), modmap),
        pl.BlockSpec((6, D_MODEL), const2),
        pl.BlockSpec((4, D_MODEL, D_MODEL), const3),
        pl.BlockSpec((D_MODEL, 2 * LORA_DECAY), const2),
        pl.BlockSpec((2, LORA_DECAY, D_MODEL), const3),
        pl.BlockSpec((2, D_MODEL), const2),
        pl.BlockSpec((D_MODEL, 2 * LORA_ICLR), const2),
        pl.BlockSpec((2, LORA_ICLR, D_MODEL), const3),
        pl.BlockSpec((2, D_MODEL), const2),
        pl.BlockSpec((1, D_MODEL), const2),
        pl.BlockSpec((1, D_MODEL), const2),
        pl.BlockSpec((1, D_MODEL), const2),
        pl.BlockSpec((D_MODEL, LANES), const2),
        pl.BlockSpec((LANES, D_MODEL), const2),
    ]
    if has_vres:
        args += [v_first, wts["v1"], wts["v2"], wts["v0"]]
        specs += [
            pl.BlockSpec((tm, D_MODEL), rowmap),
            pl.BlockSpec((D_MODEL, LORA_VRES), const2),
            pl.BlockSpec((LORA_VRES, D_MODEL), const2),
            pl.BlockSpec((1, D_MODEL), const2),
        ]
    n_out = 11
    out_dtypes = [F32] * 9 + [BF16, F32]
    return pl.pallas_call(
        functools.partial(_rwkv_feat_kernel, grid_shift=grid_shift, tiles_per_seq=tps, has_vres=has_vres),
        grid=(rows // tm,),
        in_specs=specs,
        out_specs=[pl.BlockSpec((tm, D_MODEL), rowmap)] * n_out,
        out_shape=[jax.ShapeDtypeStruct((rows, D_MODEL), dt) for dt in out_dtypes],
        compiler_params=_cparams(("parallel",)),
        name="rwkv_features",
    )(*args)


NN = (((1,), (0,)), ((), ()))
NT = (((1,), (1,)), ((), ()))
TN = (((0,), (0,)), ((), ()))
ONE_PASS = (1, 1)
LOG_DECAY_PASSES = (1, 2)


def _parts(x, n):
    out = []
    rem = x
    for i in range(n):
        p = rem.astype(BF16)
        out.append(p)
        if i + 1 < n:
            rem = rem - p.astype(F32)
    return tuple(out)


def _mm(a, b, dn, passes=ONE_PASS):
    na, nb = passes
    ap, bp = _parts(a, na), _parts(b, nb)
    lim = max(na, nb)
    acc = None
    for i, x in enumerate(ap):
        for j, y in enumerate(bp):
            if i + j < lim:
                t = lax.dot_general(x, y, dn, preferred_element_type=F32)
                acc = t if acc is None else acc + t
    return acc


def _stack_heads(x):
    first_head = lax.broadcasted_iota(jnp.int32, x.shape, 1) < HEAD_DIM
    zero = jnp.zeros_like(x)
    return jnp.concatenate([jnp.where(first_head, x, zero), jnp.where(first_head, zero, x)], axis=0).astype(BF16)


def _heads_to_rows(x):
    return jnp.concatenate([x[:, :HEAD_DIM], x[:, HEAD_DIM:]], axis=0)


def _wkv_pre_units(units, rev):
    cl = CHUNK
    nu = range(len(units))
    ti = lax.broadcasted_iota(jnp.int32, (cl, cl), 0)
    si = lax.broadcasted_iota(jnp.int32, (cl, cl), 1)
    tri = ((si >= ti) if rev else (si <= ti)).astype(BF16)
    rt = lax.broadcasted_iota(jnp.int32, (cl, LANES), 0)
    ct = lax.broadcasted_iota(jnp.int32, (cl, LANES), 1) % cl
    eye = (rt == ct).astype(F32)
    strict = (ct > rt) if rev else (ct < rt)
    incl = (ct >= rt) if rev else (ct <= rt)
    stack = _stack_heads

    cum = [_mm(tri, u[5], NN, LOG_DECAY_PASSES) for u in units]
    tot = [jnp.sum(u[5], axis=0, keepdims=True) for u in units]
    a_t = [(-units[i][3] * jnp.exp(cum[i] - units[i][5])).astype(BF16) for i in nu]
    r_t = [(units[i][0] * jnp.exp(cum[i])).astype(BF16) for i in nu]
    e_neg = [jnp.exp(-cum[i]) for i in nu]
    bk_h = [jnp.concatenate([stack(units[i][4] * e_neg[i]), stack(units[i][1] * e_neg[i])], axis=0) for i in nu]
    e_rest = [jnp.exp(tot[i] - cum[i]) for i in nu]
    b_r = [units[i][4] * e_rest[i] for i in nu]
    k_r = [units[i][1] * e_rest[i] for i in nu]
    v_s = [stack(u[2]) for u in units]

    scores = [_mm(jnp.concatenate([a_t[i], r_t[i]], axis=0), bk_h[i], NT) for i in nu]
    a_ab = [jnp.where(strict, sc[:cl, :LANES], 0.0) for sc in scores]
    a_ak = [jnp.where(strict, sc[:cl, LANES:], 0.0).astype(BF16) for sc in scores]
    a_r = [jnp.concatenate([jnp.where(incl, sc[cl:, :LANES], 0.0), jnp.where(incl, sc[cl:, LANES:], 0.0)],
                           axis=1).astype(BF16) for sc in scores]

    def same_block(s):
        return (rt // s) == (ct // s)

    inv = [eye + jnp.where(same_block(2), x, 0.0) for x in a_ab]
    s = 2
    while s < cl:
        off = same_block(2 * s) & jnp.logical_not(same_block(s))
        xs = [_mm(jnp.where(off, a_ab[i], 0.0), stack(inv[i]), NN) for i in nu]
        inv = [inv[i] + _mm(inv[i], stack(xs[i]), NN) for i in nu]
        s *= 2

    x2 = [_mm(a_ak[i], v_s[i], NN) for i in nu]
    tu = [_mm(inv[i], jnp.concatenate([stack(a_t[i]), stack(x2[i])], axis=1), NN) for i in nu]
    uh = [t[:, :LANES] for t in tu]
    u0 = [t[:, LANES:] for t in tu]
    m = [eye * jnp.exp(tot[i]) + _mm(_heads_to_rows(uh[i]), stack(b_r[i]), TN) for i in nu]
    g = [_mm(jnp.concatenate([_heads_to_rows(u0[i]), _heads_to_rows(units[i][2])], axis=0),
             jnp.concatenate([stack(b_r[i]), stack(k_r[i])], axis=0), TN) for i in nu]
    q = [r_t[i].astype(F32) + _mm(a_r[i][:, :LANES], stack(uh[i]), NN) for i in nu]
    y0 = [_mm(a_r[i], jnp.concatenate([stack(u0[i]), v_s[i]], axis=0), NN) for i in nu]
    return [(m[i], g[i], q[i], y0[i]) for i in nu]


def _wkv_pre_kernel(rc, kc, vc, ac, bc, wc, rl, kl, vl, al, bl, wl, m_ref, g_ref, q_ref, y0_ref, *, rev, cpb, ppb):
    j = pl.program_id(2)

    def run(src):
        ids = [(p, c) for p in range(ppb) for c in range(cpb)]
        units = [tuple(ref[0, pl.ds(c * CHUNK, CHUNK), pl.ds(p * LANES, LANES)] for ref in src) for p, c in ids]
        for (p, c), (m, g, q, y0) in zip(ids, _wkv_pre_units(units, rev)):
            m_ref[0, p, c] = m.astype(m_ref.dtype)
            g_ref[0, p, c] = g.astype(g_ref.dtype)
            q_ref[0, p, c] = q.astype(q_ref.dtype)
            y0_ref[0, p, c] = y0.astype(y0_ref.dtype)

    @pl.when(j == 0)
    def _():
        run((rc, kc, vc, ac, bc, wc))

    @pl.when(j > 0)
    def _():
        run((rl, kl, vl, al, bl, wl))


WKV_PAIRS_PER_BLOCK = 8


def _wkv_pre(ctx_ops, lat_ops, nb, t_ctx, t_lat, rev):
    bt = t_ctx
    cpb = bt // CHUNK
    ppb = WKV_PAIRS_PER_BLOCK
    nlb = t_lat // bt
    nch = (t_ctx + t_lat) // CHUNK
    latmap = lambda b, p, j: (b, jnp.maximum(j - 1, 0), p)
    ctxmap = lambda b, p, j: (b, 0, p)
    outmap = lambda b, p, j: (b, p, j, 0, 0)
    blk = (1, bt, ppb * LANES)
    return pl.pallas_call(
        functools.partial(_wkv_pre_kernel, rev=rev, cpb=cpb, ppb=ppb),
        grid=(nb, N_PAIRS // ppb, nlb + 1),
        in_specs=[pl.BlockSpec(blk, ctxmap)] * 6 + [pl.BlockSpec(blk, latmap)] * 6,
        out_specs=[pl.BlockSpec((1, ppb, cpb, CHUNK, LANES), outmap)] * 4,
        out_shape=[jax.ShapeDtypeStruct((nb, N_PAIRS, nch, CHUNK, LANES), BF16)] * 4,
        compiler_params=_cparams(("parallel", "parallel", "parallel")),
        name="wkv_pre_bwd" if rev else "wkv_pre_fwd",
    )(*ctx_ops, *lat_ops)


def _wkv_scan_kernel(m_ref, g_ref, q_ref, y0_ref, yc_ref, yl_ref, s_ref, *, rev, cpb, nb):
    j = pl.program_id(0)

    @pl.when(j == 0)
    def _():
        s_ref[...] = jnp.zeros_like(s_ref)

    def run(y_ref):
        order = range(cpb - 1, -1, -1) if rev else range(cpb)
        for c in order:
            for b in range(nb):
                for p in range(N_PAIRS):
                    n = b * N_PAIRS + p
                    s = s_ref[n]
                    s_hi = s.astype(BF16)
                    s_lo = (s - s_hi.astype(F32)).astype(BF16)
                    q = q_ref[b, p, c]
                    y = _mm(q, _stack_heads(s_hi), NT) + _mm(q, _stack_heads(s_lo), NT) + y0_ref[b, p, c].astype(F32)
                    y_ref[b, c * CHUNK:(c + 1) * CHUNK, p * LANES:(p + 1) * LANES] = y
                    m = _stack_heads(m_ref[b, p, c])
                    s_ref[n] = _mm(s_hi, m, NN) + _mm(s_lo, m, NN) + g_ref[b, p, c].astype(F32)

    @pl.when(j == 0)
    def _():
        run(yc_ref)

    @pl.when(j > 0)
    def _():
        run(yl_ref)


def _wkv_scan(m, g, q, y0, nb, t_ctx, t_lat, rev):
    bt = t_ctx
    cpb = bt // CHUNK
    nlb = t_lat // bt
    if rev:
        blkmap = lambda j: (0, 0, jnp.where(j == 0, 0, nlb + 1 - j), 0, 0)
        latmap = lambda j: (0, jnp.where(j == 0, nlb - 1, nlb - j), 0)
    else:
        blkmap = lambda j: (0, 0, j, 0, 0)
        latmap = lambda j: (0, jnp.maximum(j - 1, 0), 0)
    return pl.pallas_call(
        functools.partial(_wkv_scan_kernel, rev=rev, cpb=cpb, nb=nb),
        grid=(nlb + 1,),
        in_specs=[pl.BlockSpec((nb, N_PAIRS, cpb, CHUNK, LANES), blkmap)] * 4,
        out_specs=[pl.BlockSpec((nb, bt, D_MODEL), lambda j: (0, 0, 0)), pl.BlockSpec((nb, bt, D_MODEL), latmap)],
        out_shape=[jax.ShapeDtypeStruct((nb, t_ctx, D_MODEL), F32), jax.ShapeDtypeStruct((nb, t_lat, D_MODEL), F32)],
        scratch_shapes=[pltpu.VMEM((nb * N_PAIRS, HEAD_DIM, LANES), F32)],
        compiler_params=_cparams(("arbitrary",)),
        name="wkv_scan_bwd" if rev else "wkv_scan_fwd",
    )(m, g, q, y0)


def _wkv(ctx_ops, lat_ops, nb, t_ctx, t_lat, rev):
    pre = _wkv_pre(ctx_ops, lat_ops, nb, t_ctx, t_lat, rev)
    return _wkv_scan(*pre, nb, t_ctx, t_lat, rev)


def _rwkv_out_kernel(yf_ref, yb_ref, bonus_ref, g_ref, lnw_ref, lnb_ref, e_ref, et_ref, wout_ref, x_ref, gpost_ref,
                     gt_ref, o_ref):
    e = e_ref[...]
    et = et_ref[...]
    y = yf_ref[...] + yb_ref[...]
    mean = _seg_bcast(_seg_sum(y, e) * (1.0 / HEAD_DIM), et)
    d = y - mean
    var = _seg_sum(d * d, e) * (1.0 / HEAD_DIM)
    rstd = _seg_bcast(lax.rsqrt(var + GN_EPS), et)
    yn = d * rstd * lnw_ref[...] + lnb_ref[...]
    t = ((yn + bonus_ref[...]) * g_ref[...].astype(F32)).astype(BF16)
    o = _dot(t, wout_ref[...])
    o_ref[...] = x_ref[...] + gt_ref[0] * _rmsnorm(o, gpost_ref[...])


def _rwkv_out(yf, yb, bonus, g, ln_w, ln_b, e, et, w_out, x, g_post, gt, t_len):
    rows = x.shape[0]
    tm = min(256, t_len)
    tps = t_len // tm
    nbm = gt.shape[0]
    modmap = (lambda i: (i // tps, 0, 0)) if nbm > 1 else (lambda i: (0, 0, 0))
    const2 = lambda i: (0, 0)
    rowmap = lambda i: (i, 0)
    rowspec = pl.BlockSpec((tm, D_MODEL), rowmap)
    vec = pl.BlockSpec((1, D_MODEL), const2)
    return pl.pallas_call(
        _rwkv_out_kernel,
        grid=(rows // tm,),
        in_specs=[rowspec, rowspec, rowspec, rowspec, vec, vec,
                  pl.BlockSpec((D_MODEL, LANES), const2), pl.BlockSpec((LANES, D_MODEL), const2),
                  pl.BlockSpec((D_MODEL, D_MODEL), const2), rowspec, vec, pl.BlockSpec((1, 1, D_MODEL), modmap)],
        out_specs=rowspec,
        out_shape=jax.ShapeDtypeStruct((rows, D_MODEL), F32),
        compiler_params=_cparams(("parallel",)),
        name="rwkv_out",
    )(yf, yb, bonus, g, ln_w, ln_b, e, et, w_out, x, g_post, gt)


def kernel(x, c, ctx, c_ctx, mod_w, mod_b, norm_pre, norm_post, f_w_in, f_w_mix, f_b_mix, f_w_out,
           r_mu, r_w_in, r_w0, r_w1, r_w2, r_a0, r_a1, r_a2, r_v0, r_v1, r_v2,
           r_k_k, r_k_a, r_r_k, r_ln_w, r_ln_b, r_w_out):
    nb, t_lat, d = x.shape
    t_ctx = ctx.shape[1]
    assert d == D_MODEL and t_lat % t_ctx == 0 and t_ctx % CHUNK == 0 and t_lat % GRID_W == 0

    cond_rows = 8
    cond = jnp.zeros((cond_rows, d), F32).at[:nb].set(c).at[nb].set(c_ctx)
    mods = _modulation(cond, mod_w, mod_b)

    e, et = _head_indicator()
    tabs_lat = _dft_tables(t_lat)
    tabs_ctx = _dft_tables(t_ctx)

    def fourier_mix(xr, t_len, tabs, sc, sh, gt, g_pre, g_post, w_in, wcs, b_mix, w_out):
        p, sz = _fourier_in(xr, t_len, g_pre, sc, sh, w_in, wcs)
        lo, hi = _fourier_out(*tabs, _dft_fold(p, t_len), p, sz, b_mix, w_out, xr, g_post, gt, t_len)
        half = t_len // 2
        return jnp.concatenate([lo.reshape(nb, half, d), hi.reshape(nb, half, d)], axis=1).reshape(nb * t_len, d)

    xl = x.reshape(nb * t_lat, d)
    xc = ctx.reshape(nb * t_ctx, d)
    v_first = None
    vec = lambda a: a.reshape(1, d)

    for i in range(DEPTH):
        last = i == DEPTH - 1
        kind, j = i % 2, i // 2
        m = mods[i]
        sh_l, sc_l, gt_l = (m[:nb, q * d:(q + 1) * d].reshape(nb, 1, d) for q in range(3))
        sh_c, sc_c, gt_c = (m[nb:nb + 1, q * d:(q + 1) * d].reshape(1, 1, d) for q in range(3))
        g_pre, g_post = vec(norm_pre[i]), vec(norm_post[i])
        if kind == 0:
            w_in = f_w_in[j].astype(BF16)
            w_out = f_w_out[j].astype(BF16)
            wcs = _fold_channel_dft(f_w_mix[j])
            b_mix = vec(f_b_mix[j])
            xl_new = fourier_mix(xl, t_lat, tabs_lat, sc_l, sh_l, gt_l, g_pre, g_post, w_in, wcs, b_mix, w_out)
            if not last:
                xc = fourier_mix(xc, t_ctx, tabs_ctx, sc_c, sh_c, gt_c, g_pre, g_post, w_in, wcs, b_mix, w_out)
            xl = xl_new
        else:
            wts = dict(
                mu=r_mu[j], w4=r_w_in[j].astype(BF16),
                w1=jnp.concatenate([r_w1[j, 0], r_w1[j, 1]], axis=1).astype(BF16), w2=r_w2[j].astype(BF16), w0=r_w0[j],
                a1=jnp.concatenate([r_a1[j, 0], r_a1[j, 1]], axis=1).astype(BF16), a2=r_a2[j].astype(BF16), a0=r_a0[j],
                k_k=vec(r_k_k[j]), k_a=vec(r_k_a[j]), r_k=vec(r_r_k[j]), e=e, et=et)
            if j > 0:
                wts.update(v1=r_v1[j - 1].astype(BF16), v2=r_v2[j - 1].astype(BF16), v0=vec(r_v0[j - 1]))
            fc = _rwkv_features(xc, t_ctx, False, g_pre, sc_c, sh_c, wts, None if j == 0 else v_first[0])
            fl = _rwkv_features(xl, t_lat, True, g_pre, sc_l, sh_l, wts, None if j == 0 else v_first[1])
            if j == 0:
                v_first = (fc[1], fl[1])
            ys = []
            for n, rev in enumerate((False, True)):
                pick = lambda f, t: tuple(a.reshape(nb, t, d) for a in (f[0], f[3 + n], f[1], f[2], f[7 + n], f[5 + n]))
                ys.append(_wkv(pick(fc, t_ctx), pick(fl, t_lat), nb, t_ctx, t_lat, rev))
            w_out = r_w_out[j].astype(BF16)
            ln_w, ln_b = vec(r_ln_w[j]), vec(r_ln_b[j])
            xl_new = _rwkv_out(ys[0][1].reshape(-1, d), ys[1][1].reshape(-1, d), fl[10], fl[9], ln_w, ln_b, e, et,
                               w_out, xl, g_post, gt_l, t_lat)
            if not last:
                xc = _rwkv_out(ys[0][0].reshape(-1, d), ys[1][0].reshape(-1, d), fc[10], fc[9], ln_w, ln_b, e, et,
                               w_out, xc, g_post, gt_c, t_ctx)
            xl = xl_new
    return xl.reshape(nb, t_lat, d)
```

```python
import functools
import math

import numpy as np
import jax
import jax.numpy as jnp
from jax import lax
from jax.experimental import pallas as pl
from jax.experimental.pallas import tpu as pltpu

F32 = jnp.float32
BF16 = jnp.bfloat16
HIGHEST = lax.Precision.HIGHEST

D_MODEL = 1024
DEPTH = 4
GRID_W = 64
HEAD_DIM = 64
N_HEADS = D_MODEL // HEAD_DIM
N_FGROUPS = 8
FGROUP_DIM = D_MODEL // N_FGROUPS
LORA_DECAY = 64
LORA_ICLR = 64
LORA_VRES = 32
RMS_EPS = 1e-6
GN_EPS = 64e-5

LANES = 128
CHUNK = 64
HEADS_PER_TILE = LANES // HEAD_DIM
N_PAIRS = D_MODEL // LANES
VMEM_LIMIT = 56 * 1024 * 1024


def _cparams(sem):
    return pltpu.CompilerParams(dimension_semantics=sem, vmem_limit_bytes=VMEM_LIMIT)


def _dot(a, b):
    return jnp.dot(a, b, preferred_element_type=F32)


def _dot_hi(a, b):
    return jnp.dot(a, b, preferred_element_type=F32, precision=HIGHEST)


def _rmsnorm(x, g):
    return x * lax.rsqrt(jnp.mean(x * x, axis=-1, keepdims=True) + RMS_EPS) * g


def _split_bf16(x):
    hi = x.astype(BF16)
    lo = (x - hi.astype(F32)).astype(BF16)
    return hi, lo


def _seg_sum(x, e):
    return _dot(x.astype(BF16), e)


def _seg_bcast(s, et, split=False):
    if not split:
        return _dot(s.astype(BF16), et)
    hi, lo = _split_bf16(s)
    return _dot(hi, et) + _dot(lo, et)


def _head_indicator():
    e = np.zeros((D_MODEL, LANES), np.float32)
    e[np.arange(D_MODEL), np.arange(D_MODEL) // HEAD_DIM] = 1.0
    return jnp.asarray(e, BF16), jnp.asarray(e.T, BF16)


def _mod_kernel(cond_ref, w_ref, b_ref, o_ref):
    cond = cond_ref[...]
    s = cond * jax.nn.sigmoid(cond)
    o_ref[0] = _dot_hi(s, w_ref[0]) + b_ref[0]


def _modulation(cond, mod_w, mod_b):
    rows = cond.shape[0]
    tn = 1536
    return pl.pallas_call(
        _mod_kernel,
        grid=(DEPTH, 3 * D_MODEL // tn),
        in_specs=[
            pl.BlockSpec((rows, D_MODEL), lambda i, j: (0, 0)),
            pl.BlockSpec((1, D_MODEL, tn), lambda i, j: (i, 0, j)),
            pl.BlockSpec((1, 1, tn), lambda i, j: (i, 0, j)),
        ],
        out_specs=pl.BlockSpec((1, rows, tn), lambda i, j: (i, 0, j)),
        out_shape=jax.ShapeDtypeStruct((DEPTH, rows, 3 * D_MODEL), F32),
        compiler_params=_cparams(("parallel", "parallel")),
        name="modulation",
    )(cond, mod_w, mod_b.reshape(DEPTH, 1, 3 * D_MODEL))


def _wcs_kernel(cc_ref, sc_ref, w_ref, o_ref):
    w = w_ref[0]
    o_ref[0, :, :FGROUP_DIM] = _dot_hi(cc_ref[...], w).astype(o_ref.dtype)
    o_ref[0, :, FGROUP_DIM:] = _dot_hi(sc_ref[...], w).astype(o_ref.dtype)


def _fold_channel_dft(w_mix):
    n = np.arange(FGROUP_DIM)
    ang = 2.0 * np.pi * ((n[:, None] * n[None, :]) % FGROUP_DIM) / FGROUP_DIM
    cc = jnp.asarray(np.cos(ang), F32)
    sc = jnp.asarray(np.sin(ang), F32)
    return pl.pallas_call(
        _wcs_kernel,
        grid=(N_FGROUPS,),
        in_specs=[
            pl.BlockSpec((FGROUP_DIM, FGROUP_DIM), lambda g: (0, 0)),
            pl.BlockSpec((FGROUP_DIM, FGROUP_DIM), lambda g: (0, 0)),
            pl.BlockSpec((1, FGROUP_DIM, FGROUP_DIM), lambda g: (g, 0, 0)),
        ],
        out_specs=pl.BlockSpec((1, FGROUP_DIM, 2 * FGROUP_DIM), lambda g: (g, 0, 0)),
        out_shape=jax.ShapeDtypeStruct((N_FGROUPS, FGROUP_DIM, 2 * FGROUP_DIM), BF16),
        compiler_params=_cparams(("parallel",)),
        name="fold_channel_dft",
    )(cc, sc, w_mix)


DFT_TILE = 512
DFT_TILE_PAD = 16


def _dft_tile(t_len):
    return min(DFT_TILE, t_len // 2)


def _dft_table_kernel(ca_ref, sa_ref, cb_ref, sb_ref, c_ref, s_ref):
    ca = ca_ref[0]
    sa = sa_ref[0]
    cb = cb_ref[...]
    sb = sb_ref[...]
    c_ref[0] = (ca * cb - sa * sb).astype(c_ref.dtype)
    s_ref[0] = (sa * cb + ca * sb).astype(s_ref.dtype)


def _angles(p, t, period):
    return (2.0 * math.pi / period) * ((p[:, None] * t[None, :]) % period).astype(F32)


def _dft_tables(t_len):
    half = t_len // 2
    tm = _dft_tile(t_len)
    rows = tm + DFT_TILE_PAD
    nti = half // tm
    tt = min(2048, half)
    t = jnp.arange(half, dtype=jnp.int32)
    ang_a = _angles(jnp.arange(nti, dtype=jnp.int32) * tm, t, t_len)
    ang_b = _angles(jnp.arange(rows, dtype=jnp.int32), t, t_len)
    ca = jnp.cos(ang_a).reshape(nti, 1, half)
    sa = jnp.sin(ang_a).reshape(nti, 1, half)
    cb = jnp.cos(ang_b)
    sb = jnp.sin(ang_b)
    return pl.pallas_call(
        _dft_table_kernel,
        grid=(half // tt, nti),
        in_specs=[
            pl.BlockSpec((1, 1, tt), lambda j, i: (i, 0, j)),
            pl.BlockSpec((1, 1, tt), lambda j, i: (i, 0, j)),
            pl.BlockSpec((rows, tt), lambda j, i: (0, j)),
            pl.BlockSpec((rows, tt), lambda j, i: (0, j)),
        ],
        out_specs=[
            pl.BlockSpec((1, rows, tt), lambda j, i: (i, 0, j)),
            pl.BlockSpec((1, rows, tt), lambda j, i: (i, 0, j)),
        ],
        out_shape=[jax.ShapeDtypeStruct((nti, rows, half), BF16)] * 2,
        compiler_params=_cparams(("parallel", "parallel")),
        name="dft_tables",
    )(ca, sa, cb, sb)


def _mirror_select(n_out, n_in, offset):
    i = lax.broadcasted_iota(jnp.int32, (n_out, n_in), 0)
    c = lax.broadcasted_iota(jnp.int32, (n_out, n_in), 1)
    return (c == offset - i).astype(BF16)


def _dft_fold_kernel(a_ref, m_ref, x_ref, o_ref):
    tf = a_ref.shape[0]
    a = a_ref[...].astype(F32)
    src = jnp.concatenate([m_ref[...], x_ref[...]], axis=0)
    mir = _dot(_mirror_select(tf, tf + DFT_TILE_PAD, tf), src)
    t_is_zero = (lax.broadcasted_iota(jnp.int32, (tf, 1), 0) == 0) & (pl.program_id(1) == 0)
    o_ref[:, :D_MODEL] = jnp.where(t_is_zero, a[:, :D_MODEL], a[:, :D_MODEL] + mir[:, :D_MODEL]).astype(o_ref.dtype)
    o_ref[:, D_MODEL:] = (a[:, D_MODEL:] - mir[:, D_MODEL:]).astype(o_ref.dtype)


def _dft_fold(p, t_len):
    rows = p.shape[0]
    nb = rows // t_len
    half = t_len // 2
    tf = _dft_tile(t_len)
    ntf = t_len // tf
    pad = DFT_TILE_PAD
    width = p.shape[1]
    return pl.pallas_call(
        _dft_fold_kernel,
        grid=(nb, half // tf),
        in_specs=[
            pl.BlockSpec((tf, width), lambda b, j: (b * ntf + j, 0)),
            pl.BlockSpec((tf, width), lambda b, j: (b * ntf + ntf - 1 - j, 0)),
            pl.BlockSpec((pad, width), lambda b, j: (b * (t_len // pad) + jnp.where(j == 0, 0, (t_len - j * tf) // pad), 0)),
        ],
        out_specs=pl.BlockSpec((tf, width), lambda b, j: (b * (half // tf) + j, 0)),
        out_shape=jax.ShapeDtypeStruct((nb * half, width), BF16),
        compiler_params=_cparams(("parallel", "parallel")),
        name="dft_fold",
    )(p, p, p)


def _fourier_in_kernel(x_ref, g_ref, sc_ref, sh_ref, win_ref, wcs_ref, p_ref, sz_ref):
    h = _rmsnorm(x_ref[...], g_ref[...]) * (1.0 + sc_ref[0]) + sh_ref[0]
    uz = _dot(h.astype(BF16), win_ref[...])
    z = uz[:, D_MODEL:]
    sz_ref[...] = (z * jax.nn.sigmoid(z)).astype(sz_ref.dtype)
    for g in range(N_FGROUPS):
        lo, hi = g * FGROUP_DIM, (g + 1) * FGROUP_DIM
        pg = _dot(uz[:, lo:hi].astype(BF16), wcs_ref[g])
        p_ref[:, lo:hi] = pg[:, :FGROUP_DIM].astype(p_ref.dtype)
        p_ref[:, D_MODEL + lo:D_MODEL + hi] = pg[:, FGROUP_DIM:].astype(p_ref.dtype)


def _fourier_in(x, t_len, g_pre, sc, sh, w_in, wcs):
    rows = x.shape[0]
    tm = min(512, t_len)
    tpb = t_len // tm
    nbm = sc.shape[0]
    modmap = (lambda i: (i // tpb, 0, 0)) if nbm > 1 else (lambda i: (0, 0, 0))
    return pl.pallas_call(
        _fourier_in_kernel,
        grid=(rows // tm,),
        in_specs=[
            pl.BlockSpec((tm, D_MODEL), lambda i: (i, 0)),
            pl.BlockSpec((1, D_MODEL), lambda i: (0, 0)),
            pl.BlockSpec((1, 1, D_MODEL), modmap),
            pl.BlockSpec((1, 1, D_MODEL), modmap),
            pl.BlockSpec((D_MODEL, 2 * D_MODEL), lambda i: (0, 0)),
            pl.BlockSpec((N_FGROUPS, FGROUP_DIM, 2 * FGROUP_DIM), lambda i: (0, 0, 0)),
        ],
        out_specs=[
            pl.BlockSpec((tm, 2 * D_MODEL), lambda i: (i, 0)),
            pl.BlockSpec((tm, D_MODEL), lambda i: (i, 0)),
        ],
        out_shape=[
            jax.ShapeDtypeStruct((rows, 2 * D_MODEL), BF16),
            jax.ShapeDtypeStruct((rows, D_MODEL), BF16),
        ],
        compiler_params=_cparams(("parallel",)),
        name="fourier_in",
    )(x, g_pre, sc, sh, w_in, wcs)


def _fourier_out_kernel(c_ref, s_ref, ec_ref, os_ref, ph_ref, szd_ref, szm_ref, bmix_ref, wout_ref, xd_ref, xm_ref,
                        gpost_ref, gt_ref, od_ref, om_ref, accc_ref, accs_ref, *, scale):
    i, k = pl.program_id(1), pl.program_id(2)

    @pl.when(k == 0)
    def _():
        accc_ref[...] = jnp.zeros_like(accc_ref)
        accs_ref[...] = jnp.zeros_like(accs_ref)

    accc_ref[...] += _dot(c_ref[0], ec_ref[...])
    accs_ref[...] += _dot(s_ref[0], os_ref[...])

    def finish(f, sz_ref, x_ref, o_ref):
        gated = ((f * scale + bmix_ref[...]) * sz_ref[...].astype(F32)).astype(BF16)
        o = _dot(gated, wout_ref[...])
        o_ref[...] = x_ref[...] + gt_ref[0] * _rmsnorm(o, gpost_ref[...])

    @pl.when(k == pl.num_programs(2) - 1)
    def _():
        rows = accc_ref.shape[0]
        tm = rows - DFT_TILE_PAD
        p = i * tm + lax.broadcasted_iota(jnp.int32, (rows, 1), 0)
        sign = (1 - 2 * (p & 1)).astype(F32)
        fc = accc_ref[...] + sign * ph_ref[0:1, :].astype(F32)
        fs = accs_ref[...]
        finish((fc - fs)[:tm], szd_ref, xd_ref, od_ref)
        hi, lo = _split_bf16(fc + fs)
        sel = _mirror_select(tm, rows, tm)
        finish(_dot(sel, hi) + _dot(sel, lo), szm_ref, xm_ref, om_ref)


def _fourier_out(c_tab, s_tab, eo, p, sz, b_mix, w_out, x, g_post, gt, t_len):
    rows = x.shape[0]
    nb = rows // t_len
    half = t_len // 2
    tm = _dft_tile(t_len)
    tk = min(1024, half)
    pad = DFT_TILE_PAD
    nti, ntk, ntf = half // tm, half // tk, t_len // tm
    nbm = gt.shape[0]
    modmap = (lambda b, i, k: (b, 0, 0)) if nbm > 1 else (lambda b, i, k: (0, 0, 0))
    direct = lambda b, i, k: (b * ntf + i, 0)
    mirror = lambda b, i, k: (b * ntf + ntf - 1 - i, 0)
    const = lambda b, i, k: (0, 0)
    rowspec = lambda m: pl.BlockSpec((tm, D_MODEL), m)
    scale = 1.0 / math.sqrt(t_len * FGROUP_DIM)
    return pl.pallas_call(
        functools.partial(_fourier_out_kernel, scale=scale),
        grid=(nb, nti, ntk),
        in_specs=[
            pl.BlockSpec((1, tm + pad, tk), lambda b, i, k: (i, 0, k)),
            pl.BlockSpec((1, tm + pad, tk), lambda b, i, k: (i, 0, k)),
            pl.BlockSpec((tk, D_MODEL), lambda b, i, k: (b * ntk + k, 0)),
            pl.BlockSpec((tk, D_MODEL), lambda b, i, k: (b * ntk + k, 1)),
            pl.BlockSpec((pad, D_MODEL), lambda b, i, k: ((b * t_len + half) // pad, 0)),
            rowspec(direct), rowspec(mirror),
            pl.BlockSpec((1, D_MODEL), const),
            pl.BlockSpec((D_MODEL, D_MODEL), const),
            rowspec(direct), rowspec(mirror),
            pl.BlockSpec((1, D_MODEL), const),
            pl.BlockSpec((1, 1, D_MODEL), modmap),
        ],
        out_specs=[pl.BlockSpec((tm, D_MODEL), lambda b, i, k: (b * nti + i, 0)),
                   pl.BlockSpec((tm, D_MODEL), lambda b, i, k: (b * nti + nti - 1 - i, 0))],
        out_shape=[jax.ShapeDtypeStruct((nb * half, D_MODEL), F32)] * 2,
        scratch_shapes=[pltpu.VMEM((tm + pad, D_MODEL), F32)] * 2,
        compiler_params=_cparams(("parallel", "parallel", "arbitrary")),
        name="fourier_out",
    )(c_tab, s_tab, eo, eo, p, sz, sz, b_mix, w_out, x, x, g_post, gt)


def _rwkv_feat_kernel(*refs, grid_shift, tiles_per_seq, has_vres):
    if grid_shift:
        x_ref, xp_ref, xn_ref = refs[:3]
        refs = refs[3:]
    else:
        x_ref = refs[0]
        refs = refs[1:]
    (g_ref, sc_ref, sh_ref, mu_ref, w4_ref, w1_ref, w2_ref, w0_ref, a1_ref, a2_ref, a0_ref,
     kk_ref, ka_ref, rk_ref, e_ref, et_ref) = refs[:16]
    refs = refs[16:]
    if has_vres:
        vf_ref, v1_ref, v2_ref, v0_ref = refs[:4]
        refs = refs[4:]
    (r_out, v_out, kkn_out, k0_out, k1_out, lw0_out, lw1_out, b0_out, b1_out, g_out, bonus_out) = refs

    g = g_ref[...]
    sc1 = 1.0 + sc_ref[0]
    sh = sh_ref[0]
    tm = x_ref.shape[0]
    h = _rmsnorm(x_ref[...], g) * sc1 + sh
    row = lax.broadcasted_iota(jnp.int32, (tm, 1), 0)
    if grid_shift:
        i = pl.program_id(0)
        first = (i % tiles_per_seq) == 0
        last = (i % tiles_per_seq) == tiles_per_seq - 1
        hp = _rmsnorm(xp_ref[...], g) * sc1 + sh
        hn = _rmsnorm(xn_ref[...], g) * sc1 + sh
        hp = jnp.where(first, 0.0, hp)
        hn = jnp.where(last, 0.0, hn)
        hall = jnp.concatenate([hp, h, hn], axis=0)
        tot = tm + 2 * GRID_W
        up = hall[0:tm]
        down = hall[2 * GRID_W:2 * GRID_W + tm]
        left = pltpu.roll(hall, 1, axis=0)[GRID_W:GRID_W + tm]
        right = pltpu.roll(hall, tot - 1, axis=0)[GRID_W:GRID_W + tm]
        col = row % GRID_W
        left = jnp.where(col == 0, 0.0, left)
        right = jnp.where(col == GRID_W - 1, 0.0, right)
        nbr = (up + down + left + right) * 0.25
    else:
        left = jnp.where(row == 0, 0.0, pltpu.roll(h, 1, axis=0))
        right = jnp.where(row == tm - 1, 0.0, pltpu.roll(h, tm - 1, axis=0))
        nbr = (left + right) * 0.5
    dlt = nbr - h

    def mix(p):
        return h + dlt * mu_ref[p:p + 1, :]

    e = e_ref[...]
    et = et_ref[...]
    tw = jnp.tanh(_dot(mix(4).astype(BF16), w1_ref[...])).astype(BF16)
    ta = _dot(mix(5).astype(BF16), a1_ref[...]).astype(BF16)
    decay_gain = -math.exp(-0.5)
    a_gate = []
    for n, lw_out in enumerate((lw0_out, lw1_out)):
        wz = w0_ref[n:n + 1, :] + _dot(tw[:, n * LORA_DECAY:(n + 1) * LORA_DECAY], w2_ref[n])
        lw_out[...] = decay_gain * jax.nn.sigmoid(wz)
        az = a0_ref[n:n + 1, :] + _dot(ta[:, n * LORA_ICLR:(n + 1) * LORA_ICLR], a2_ref[n])
        a_gate.append(jax.nn.sigmoid(az))

    gz = _dot(mix(3).astype(BF16), w4_ref[3])
    g_out[...] = (gz * jax.nn.sigmoid(gz)).astype(g_out.dtype)

    k = _dot(mix(1).astype(BF16), w4_ref[1])
    kk = k * kk_ref[...]
    n2 = _seg_sum(kk * kk, e)
    inv = 1.0 / jnp.maximum(jnp.sqrt(n2), 1e-12)
    kkn = kk * _seg_bcast(inv, et, split=True)
    kkn_out[...] = kkn
    kd = []
    for a, k_out, b_out in zip(a_gate, (k0_out, k1_out), (b0_out, b1_out)):
        kd.append(k * (1.0 + (a - 1.0) * ka_ref[...]))
        k_out[...] = kd[-1]
        b_out[...] = kkn * a

    xv = mix(2).astype(BF16)
    v = _dot(xv, w4_ref[2])
    if has_vres:
        vz = v0_ref[...] + _dot(_dot(xv, v1_ref[...]).astype(BF16), v2_ref[...])
        v = v + (vf_ref[...] - v) * jax.nn.sigmoid(vz)
    v_out[...] = v

    r = _dot(mix(0).astype(BF16), w4_ref[0])
    r_out[...] = r
    rsum = r * (kd[0] + kd[1]) * rk_ref[...]
    bonus_out[...] = _seg_bcast(_seg_sum(rsum, e), et) * v


def _rwkv_features(x, t_len, grid_shift, g_pre, sc, sh, wts, v_first):
    rows = x.shape[0]
    tm = 256
    tps = t_len // tm
    assert grid_shift or tps == 1
    nbm = sc.shape[0]
    modmap = (lambda i: (i // tps, 0, 0)) if nbm > 1 else (lambda i: (0, 0, 0))
    const2 = lambda i: (0, 0)
    const3 = lambda i: (0, 0, 0)
    rowmap = lambda i: (i, 0)
    has_vres = v_first is not None
    hb = tm // GRID_W
    nhalo = rows // GRID_W

    args, specs = [x], [pl.BlockSpec((tm, D_MODEL), rowmap)]
    if grid_shift:
        args += [x, x]
        specs += [
            pl.BlockSpec((GRID_W, D_MODEL), lambda i: (jnp.maximum(i * hb - 1, 0), 0)),
            pl.BlockSpec((GRID_W, D_MODEL), lambda i: (jnp.minimum((i + 1) * hb, nhalo - 1), 0)),
        ]
    args += [g_pre, sc, sh, wts["mu"], wts["w4"], wts["w1"], wts["w2"], wts["w0"], wts["a1"], wts["a2"], wts["a0"],
             wts["k_k"], wts["k_a"], wts["r_k"], wts["e"], wts["et"]]
    specs += [
        pl.BlockSpec((1, D_MODEL), const2),
        pl.BlockSpec((1, 1, D_MODEL), modmap),
        pl.BlockSpec((1, 1, D_MODEL), modmap),
        pl.BlockSpec((6, D_MODEL), const2),
        pl.BlockSpec((4, D_MODEL, D_MODEL), const3),
        pl.BlockSpec((D_MODEL, 2 * LORA_DECAY), const2),
        pl.BlockSpec((2, LORA_DECAY, D_MODEL), const3),
        pl.BlockSpec((2, D_MODEL), const2),
        pl.BlockSpec((D_MODEL, 2 * LORA_ICLR), const2),
        pl.BlockSpec((2, LORA_ICLR, D_MODEL), const3),
        pl.BlockSpec((2, D_MODEL), const2),
        pl.BlockSpec((1, D_MODEL), const2),
        pl.BlockSpec((1, D_MODEL), const2),
        pl.BlockSpec((1, D_MODEL), const2),
        pl.BlockSpec((D_MODEL, LANES), const2),
        pl.BlockSpec((LANES, D_MODEL), const2),
    ]
    if has_vres:
        args += [v_first, wts["v1"], wts["v2"], wts["v0"]]
        specs += [
            pl.BlockSpec((tm, D_MODEL), rowmap),
            pl.BlockSpec((D_MODEL, LORA_VRES), const2),
            pl.BlockSpec((LORA_VRES, D_MODEL), const2),
            pl.BlockSpec((1, D_MODEL), const2),
        ]
    n_out = 11
    out_dtypes = [F32] * 9 + [BF16, F32]
    return pl.pallas_call(
        functools.partial(_rwkv_feat_kernel, grid_shift=grid_shift, tiles_per_seq=tps, has_vres=has_vres),
        grid=(rows // tm,),
        in_specs=specs,
        out_specs=[pl.BlockSpec((tm, D_MODEL), rowmap)] * n_out,
        out_shape=[jax.ShapeDtypeStruct((rows, D_MODEL), dt) for dt in out_dtypes],
        compiler_params=_cparams(("parallel",)),
        name="rwkv_features",
    )(*args)


NN = (((1,), (0,)), ((), ()))
NT = (((1,), (1,)), ((), ()))
TN = (((0,), (0,)), ((), ()))
ONE_PASS = (1, 1)
LOG_DECAY_PASSES = (1, 2)


def _parts(x, n):
    out = []
    rem = x
    for i in range(n):
        p = rem.astype(BF16)
        out.append(p)
        if i + 1 < n:
            rem = rem - p.astype(F32)
    return tuple(out)


def _mm(a, b, dn, passes=ONE_PASS):
    na, nb = passes
    ap, bp = _parts(a, na), _parts(b, nb)
    lim = max(na, nb)
    acc = None
    for i, x in enumerate(ap):
        for j, y in enumerate(bp):
            if i + j < lim:
                t = lax.dot_general(x, y, dn, preferred_element_type=F32)
                acc = t if acc is None else acc + t
    return acc


def _stack_heads(x):
    first_head = lax.broadcasted_iota(jnp.int32, x.shape, 1) < HEAD_DIM
    zero = jnp.zeros_like(x)
    return jnp.concatenate([jnp.where(first_head, x, zero), jnp.where(first_head, zero, x)], axis=0).astype(BF16)


def _heads_to_rows(x):
    return jnp.concatenate([x[:, :HEAD_DIM], x[:, HEAD_DIM:]], axis=0)


def _wkv_pre_units(units, rev):
    cl = CHUNK
    nu = range(len(units))
    ti = lax.broadcasted_iota(jnp.int32, (cl, cl), 0)
    si = lax.broadcasted_iota(jnp.int32, (cl, cl), 1)
    tri = ((si >= ti) if rev else (si <= ti)).astype(BF16)
    rt = lax.broadcasted_iota(jnp.int32, (cl, LANES), 0)
    ct = lax.broadcasted_iota(jnp.int32, (cl, LANES), 1) % cl
    eye = (rt == ct).astype(F32)
    strict = (ct > rt) if rev else (ct < rt)
    incl = (ct >= rt) if rev else (ct <= rt)
    stack = _stack_heads

    cum = [_mm(tri, u[5], NN, LOG_DECAY_PASSES) for u in units]
    tot = [jnp.sum(u[5], axis=0, keepdims=True) for u in units]
    a_t = [(-units[i][3] * jnp.exp(cum[i] - units[i][5])).astype(BF16) for i in nu]
    r_t = [(units[i][0] * jnp.exp(cum[i])).astype(BF16) for i in nu]
    e_neg = [jnp.exp(-cum[i]) for i in nu]
    bk_h = [jnp.concatenate([stack(units[i][4] * e_neg[i]), stack(units[i][1] * e_neg[i])], axis=0) for i in nu]
    e_rest = [jnp.exp(tot[i] - cum[i]) for i in nu]
    b_r = [units[i][4] * e_rest[i] for i in nu]
    k_r = [units[i][1] * e_rest[i] for i in nu]
    v_s = [stack(u[2]) for u in units]

    scores = [_mm(jnp.concatenate([a_t[i], r_t[i]], axis=0), bk_h[i], NT) for i in nu]
    a_ab = [jnp.where(strict, sc[:cl, :LANES], 0.0) for sc in scores]
    a_ak = [jnp.where(strict, sc[:cl, LANES:], 0.0).astype(BF16) for sc in scores]
    a_r = [jnp.concatenate([jnp.where(incl, sc[cl:, :LANES], 0.0), jnp.where(incl, sc[cl:, LANES:], 0.0)],
                           axis=1).astype(BF16) for sc in scores]

    def same_block(s):
        return (rt // s) == (ct // s)

    inv = [eye + jnp.where(same_block(2), x, 0.0) for x in a_ab]
    s = 2
    while s < cl:
        off = same_block(2 * s) & jnp.logical_not(same_block(s))
        xs = [_mm(jnp.where(off, a_ab[i], 0.0), stack(inv[i]), NN) for i in nu]
        inv = [inv[i] + _mm(inv[i], stack(xs[i]), NN) for i in nu]
        s *= 2

    x2 = [_mm(a_ak[i], v_s[i], NN) for i in nu]
    tu = [_mm(inv[i], jnp.concatenate([stack(a_t[i]), stack(x2[i])], axis=1), NN) for i in nu]
    uh = [t[:, :LANES] for t in tu]
    u0 = [t[:, LANES:] for t in tu]
    uh_s = [stack(x) for x in uh]
    u0v_s = [jnp.concatenate([stack(u0[i]), v_s[i]], axis=0) for i in nu]
    m = [eye * jnp.exp(tot[i]) + _mm(_heads_to_rows(b_r[i]), uh_s[i], TN) for i in nu]
    g = [_mm(jnp.concatenate([_heads_to_rows(b_r[i]), _heads_to_rows(k_r[i])], axis=0), u0v_s[i], TN) for i in nu]
    q = [r_t[i].astype(F32) + _mm(a_r[i][:, :LANES], uh_s[i], NN) for i in nu]
    y0 = [_mm(a_r[i], u0v_s[i], NN) for i in nu]
    return [(m[i], g[i], q[i], y0[i]) for i in nu]


def _wkv_pre_kernel(rc, kc, vc, ac, bc, wc, rl, kl, vl, al, bl, wl, m_ref, g_ref, q_ref, y0_ref, *, rev, cpb, ppb):
    j = pl.program_id(2)

    def run(src):
        ids = [(p, c) for p in range(ppb) for c in range(cpb)]
        units = [tuple(ref[0, pl.ds(c * CHUNK, CHUNK), pl.ds(p * LANES, LANES)] for ref in src) for p, c in ids]
        for (p, c), (m, g, q, y0) in zip(ids, _wkv_pre_units(units, rev)):
            m_ref[0, p, c] = m.astype(m_ref.dtype)
            g_ref[0, p, c] = g.astype(g_ref.dtype)
            q_ref[0, p, c] = q.astype(q_ref.dtype)
            y0_ref[0, p, c] = y0.astype(y0_ref.dtype)

    @pl.when(j == 0)
    def _():
        run((rc, kc, vc, ac, bc, wc))

    @pl.when(j > 0)
    def _():
        run((rl, kl, vl, al, bl, wl))


WKV_PAIRS_PER_BLOCK = 8


def _wkv_pre(ctx_ops, lat_ops, nb, t_ctx, t_lat, rev):
    bt = t_ctx
    cpb = bt // CHUNK
    ppb = WKV_PAIRS_PER_BLOCK
    nlb = t_lat // bt
    nch = (t_ctx + t_lat) // CHUNK
    latmap = lambda b, p, j: (b, jnp.maximum(j - 1, 0), p)
    ctxmap = lambda b, p, j: (b, 0, p)
    outmap = lambda b, p, j: (b, p, j, 0, 0)
    blk = (1, bt, ppb * LANES)
    return pl.pallas_call(
        functools.partial(_wkv_pre_kernel, rev=rev, cpb=cpb, ppb=ppb),
        grid=(nb, N_PAIRS // ppb, nlb + 1),
        in_specs=[pl.BlockSpec(blk, ctxmap)] * 6 + [pl.BlockSpec(blk, latmap)] * 6,
        out_specs=[pl.BlockSpec((1, ppb, cpb, CHUNK, LANES), outmap)] * 4,
        out_shape=[jax.ShapeDtypeStruct((nb, N_PAIRS, nch, CHUNK, LANES), BF16)] * 4,
        compiler_params=_cparams(("parallel", "parallel", "parallel")),
        name="wkv_pre_bwd" if rev else "wkv_pre_fwd",
    )(*ctx_ops, *lat_ops)


def _wkv_scan_kernel(m_ref, g_ref, q_ref, y0_ref, yc_ref, yl_ref, s_ref, *, rev, cpb, nb):
    j = pl.program_id(0)

    @pl.when(j == 0)
    def _():
        s_ref[...] = jnp.zeros_like(s_ref)

    def run(y_ref):
        order = range(cpb - 1, -1, -1) if rev else range(cpb)
        for c in order:
            for b in range(nb):
                for p in range(N_PAIRS):
                    n = b * N_PAIRS + p
                    s = s_ref[n]
                    s_hi = s.astype(BF16)
                    s_lo = (s - s_hi.astype(F32)).astype(BF16)
                    qm = jnp.concatenate([q_ref[b, p, c], m_ref[b, p, c]], axis=0)
                    out = _mm(qm, _stack_heads(s_hi), NN) + _mm(qm, _stack_heads(s_lo), NN)
                    y_ref[b, c * CHUNK:(c + 1) * CHUNK, p * LANES:(p + 1) * LANES] = (
                        out[:CHUNK] + y0_ref[b, p, c].astype(F32))
                    s_ref[n] = out[CHUNK:] + g_ref[b, p, c].astype(F32)

    @pl.when(j == 0)
    def _():
        run(yc_ref)

    @pl.when(j > 0)
    def _():
        run(yl_ref)


def _wkv_scan(m, g, q, y0, nb, t_ctx, t_lat, rev):
    bt = t_ctx
    cpb = bt // CHUNK
    nlb = t_lat // bt
    if rev:
        blkmap = lambda j: (0, 0, jnp.where(j == 0, 0, nlb + 1 - j), 0, 0)
        latmap = lambda j: (0, jnp.where(j == 0, nlb - 1, nlb - j), 0)
    else:
        blkmap = lambda j: (0, 0, j, 0, 0)
        latmap = lambda j: (0, jnp.maximum(j - 1, 0), 0)
    return pl.pallas_call(
        functools.partial(_wkv_scan_kernel, rev=rev, cpb=cpb, nb=nb),
        grid=(nlb + 1,),
        in_specs=[pl.BlockSpec((nb, N_PAIRS, cpb, CHUNK, LANES), blkmap)] * 4,
        out_specs=[pl.BlockSpec((nb, bt, D_MODEL), lambda j: (0, 0, 0)), pl.BlockSpec((nb, bt, D_MODEL), latmap)],
        out_shape=[jax.ShapeDtypeStruct((nb, t_ctx, D_MODEL), F32), jax.ShapeDtypeStruct((nb, t_lat, D_MODEL), F32)],
        scratch_shapes=[pltpu.VMEM((nb * N_PAIRS, HEAD_DIM, LANES), F32)],
        compiler_params=_cparams(("arbitrary",)),
        name="wkv_scan_bwd" if rev else "wkv_scan_fwd",
    )(m, g, q, y0)


def _wkv(ctx_ops, lat_ops, nb, t_ctx, t_lat, rev):
    pre = _wkv_pre(ctx_ops, lat_ops, nb, t_ctx, t_lat, rev)
    return _wkv_scan(*pre, nb, t_ctx, t_lat, rev)


def _rwkv_out_kernel(yf_ref, yb_ref, bonus_ref, g_ref, lnw_ref, lnb_ref, e_ref, et_ref, wout_ref, x_ref, gpost_ref,
                     gt_ref, o_ref):
    e = e_ref[...]
    et = et_ref[...]
    y = yf_ref[...] + yb_ref[...]
    y_hi, y_lo = _split_bf16(y)
    mean = _seg_bcast((_dot(y_hi, e) + _dot(y_lo, e)) * (1.0 / HEAD_DIM), et, split=True)
    d = y - mean
    var = _seg_sum(d * d, e) * (1.0 / HEAD_DIM)
    rstd = _seg_bcast(lax.rsqrt(var + GN_EPS), et)
    yn = d * rstd * lnw_ref[...] + lnb_ref[...]
    t = ((yn + bonus_ref[...]) * g_ref[...].astype(F32)).astype(BF16)
    o = _dot(t, wout_ref[...])
    o_ref[...] = x_ref[...] + gt_ref[0] * _rmsnorm(o, gpost_ref[...])


def _rwkv_out(yf, yb, bonus, g, ln_w, ln_b, e, et, w_out, x, g_post, gt, t_len):
    rows = x.shape[0]
    tm = min(256, t_len)
    tps = t_len // tm
    nbm = gt.shape[0]
    modmap = (lambda i: (i // tps, 0, 0)) if nbm > 1 else (lambda i: (0, 0, 0))
    const2 = lambda i: (0, 0)
    rowmap = lambda i: (i, 0)
    rowspec = pl.BlockSpec((tm, D_MODEL), rowmap)
    vec = pl.BlockSpec((1, D_MODEL), const2)
    return pl.pallas_call(
        _rwkv_out_kernel,
        grid=(rows // tm,),
        in_specs=[rowspec, rowspec, rowspec, rowspec, vec, vec,
                  pl.BlockSpec((D_MODEL, LANES), const2), pl.BlockSpec((LANES, D_MODEL), const2),
                  pl.BlockSpec((D_MODEL, D_MODEL), const2), rowspec, vec, pl.BlockSpec((1, 1, D_MODEL), modmap)],
        out_specs=rowspec,
        out_shape=jax.ShapeDtypeStruct((rows, D_MODEL), F32),
        compiler_params=_cparams(("parallel",)),
        name="rwkv_out",
    )(yf, yb, bonus, g, ln_w, ln_b, e, et, w_out, x, g_post, gt)


def kernel(x, c, ctx, c_ctx, mod_w, mod_b, norm_pre, norm_post, f_w_in, f_w_mix, f_b_mix, f_w_out,
           r_mu, r_w_in, r_w0, r_w1, r_w2, r_a0, r_a1, r_a2, r_v0, r_v1, r_v2,
           r_k_k, r_k_a, r_r_k, r_ln_w, r_ln_b, r_w_out):
    nb, t_lat, d = x.shape
    t_ctx = ctx.shape[1]
    assert d == D_MODEL and t_lat % t_ctx == 0 and t_ctx % CHUNK == 0 and t_lat % GRID_W == 0

    cond_rows = 8
    cond = jnp.zeros((cond_rows, d), F32).at[:nb].set(c).at[nb].set(c_ctx)
    mods = _modulation(cond, mod_w, mod_b)

    e, et = _head_indicator()
    tabs_lat = _dft_tables(t_lat)
    tabs_ctx = _dft_tables(t_ctx)

    def fourier_mix(xr, t_len, tabs, sc, sh, gt, g_pre, g_post, w_in, wcs, b_mix, w_out):
        p, sz = _fourier_in(xr, t_len, g_pre, sc, sh, w_in, wcs)
        lo, hi = _fourier_out(*tabs, _dft_fold(p, t_len), p, sz, b_mix, w_out, xr, g_post, gt, t_len)
        half = t_len // 2
        return jnp.concatenate([lo.reshape(nb, half, d), hi.reshape(nb, half, d)], axis=1).reshape(nb * t_len, d)

    xl = x.reshape(nb * t_lat, d)
    xc = ctx.reshape(nb * t_ctx, d)
    v_first = None
    vec = lambda a: a.reshape(1, d)

    for i in range(DEPTH):
        last = i == DEPTH - 1
        kind, j = i % 2, i // 2
        m = mods[i]
        sh_l, sc_l, gt_l = (m[:nb, q * d:(q + 1) * d].reshape(nb, 1, d) for q in range(3))
        sh_c, sc_c, gt_c = (m[nb:nb + 1, q * d:(q + 1) * d].reshape(1, 1, d) for q in range(3))
        g_pre, g_post = vec(norm_pre[i]), vec(norm_post[i])
        if kind == 0:
            w_in = f_w_in[j].astype(BF16)
            w_out = f_w_out[j].astype(BF16)
            wcs = _fold_channel_dft(f_w_mix[j])
            b_mix = vec(f_b_mix[j])
            xl_new = fourier_mix(xl, t_lat, tabs_lat, sc_l, sh_l, gt_l, g_pre, g_post, w_in, wcs, b_mix, w_out)
            if not last:
                xc = fourier_mix(xc, t_ctx, tabs_ctx, sc_c, sh_c, gt_c, g_pre, g_post, w_in, wcs, b_mix, w_out)
            xl = xl_new
        else:
            wts = dict(
                mu=r_mu[j], w4=r_w_in[j].astype(BF16),
                w1=jnp.concatenate([r_w1[j, 0], r_w1[j, 1]], axis=1).astype(BF16), w2=r_w2[j].astype(BF16), w0=r_w0[j],
                a1=jnp.concatenate([r_a1[j, 0], r_a1[j, 1]], axis=1).astype(BF16), a2=r_a2[j].astype(BF16), a0=r_a0[j],
                k_k=vec(r_k_k[j]), k_a=vec(r_k_a[j]), r_k=vec(r_r_k[j]), e=e, et=et)
            if j > 0:
                wts.update(v1=r_v1[j - 1].astype(BF16), v2=r_v2[j - 1].astype(BF16), v0=vec(r_v0[j - 1]))
            fc = _rwkv_features(xc, t_ctx, False, g_pre, sc_c, sh_c, wts, None if j == 0 else v_first[0])
            fl = _rwkv_features(xl, t_lat, True, g_pre, sc_l, sh_l, wts, None if j == 0 else v_first[1])
            if j == 0:
                v_first = (fc[1], fl[1])
            ys = []
            for n, rev in enumerate((False, True)):
                pick = lambda f, t: tuple(a.reshape(nb, t, d) for a in (f[0], f[3 + n], f[1], f[2], f[7 + n], f[5 + n]))
                ys.append(_wkv(pick(fc, t_ctx), pick(fl, t_lat), nb, t_ctx, t_lat, rev))
            w_out = r_w_out[j].astype(BF16)
            ln_w, ln_b = vec(r_ln_w[j]), vec(r_ln_b[j])
            xl_new = _rwkv_out(ys[0][1].reshape(-1, d), ys[1][1].reshape(-1, d), fl[10], fl[9], ln_w, ln_b, e, et,
                               w_out, xl, g_post, gt_l, t_lat)
            if not last:
                xc = _rwkv_out(ys[0][0].reshape(-1, d), ys[1][0].reshape(-1, d), fc[10], fc[9], ln_w, ln_b, e, et,
                               w_out, xc, g_post, gt_c, t_ctx)
            xl = xl_new
    return xl.reshape(nb, t_lat, d)
```

```python
import functools
import math

import numpy as np
import jax
import jax.numpy as jnp
from jax import lax
from jax.experimental import pallas as pl
from jax.experimental.pallas import tpu as pltpu

F32 = jnp.float32
BF16 = jnp.bfloat16
HIGHEST = lax.Precision.HIGHEST

D_MODEL = 1024
DEPTH = 4
GRID_W = 64
HEAD_DIM = 64
N_HEADS = D_MODEL // HEAD_DIM
N_FGROUPS = 8
FGROUP_DIM = D_MODEL // N_FGROUPS
LORA_DECAY = 64
LORA_ICLR = 64
LORA_VRES = 32
RMS_EPS = 1e-6
GN_EPS = 64e-5

LANES = 128
CHUNK = 64
HEADS_PER_TILE = LANES // HEAD_DIM
N_PAIRS = D_MODEL // LANES
VMEM_LIMIT = 56 * 1024 * 1024


def _cparams(sem):
    return pltpu.CompilerParams(dimension_semantics=sem, vmem_limit_bytes=VMEM_LIMIT)


def _dot(a, b):
    return jnp.dot(a, b, preferred_element_type=F32)


def _dot_hi(a, b):
    return jnp.dot(a, b, preferred_element_type=F32, precision=HIGHEST)


def _rmsnorm(x, g):
    return x * lax.rsqrt(jnp.mean(x * x, axis=-1, keepdims=True) + RMS_EPS) * g


def _split_bf16(x):
    hi = x.astype(BF16)
    lo = (x - hi.astype(F32)).astype(BF16)
    return hi, lo


def _seg_sum(x, e):
    return _dot(x.astype(BF16), e)


def _seg_bcast(s, et, split=False):
    if not split:
        return _dot(s.astype(BF16), et)
    hi, lo = _split_bf16(s)
    return _dot(hi, et) + _dot(lo, et)


def _head_indicator():
    e = np.zeros((D_MODEL, LANES), np.float32)
    e[np.arange(D_MODEL), np.arange(D_MODEL) // HEAD_DIM] = 1.0
    return jnp.asarray(e, BF16), jnp.asarray(e.T, BF16)


def _mod_kernel(cond_ref, w_ref, b_ref, o_ref):
    cond = cond_ref[...]
    s = cond * jax.nn.sigmoid(cond)
    o_ref[0] = _dot_hi(s, w_ref[0]) + b_ref[0]


def _modulation(cond, mod_w, mod_b):
    rows = cond.shape[0]
    tn = 1536
    return pl.pallas_call(
        _mod_kernel,
        grid=(DEPTH, 3 * D_MODEL // tn),
        in_specs=[
            pl.BlockSpec((rows, D_MODEL), lambda i, j: (0, 0)),
            pl.BlockSpec((1, D_MODEL, tn), lambda i, j: (i, 0, j)),
            pl.BlockSpec((1, 1, tn), lambda i, j: (i, 0, j)),
        ],
        out_specs=pl.BlockSpec((1, rows, tn), lambda i, j: (i, 0, j)),
        out_shape=jax.ShapeDtypeStruct((DEPTH, rows, 3 * D_MODEL), F32),
        compiler_params=_cparams(("parallel", "parallel")),
        name="modulation",
    )(cond, mod_w, mod_b.reshape(DEPTH, 1, 3 * D_MODEL))


def _wcs_kernel(cc_ref, sc_ref, w_ref, o_ref):
    w = w_ref[0]
    o_ref[0, :, :FGROUP_DIM] = _dot_hi(cc_ref[...], w).astype(o_ref.dtype)
    o_ref[0, :, FGROUP_DIM:] = _dot_hi(sc_ref[...], w).astype(o_ref.dtype)


def _fold_channel_dft(w_mix):
    n = np.arange(FGROUP_DIM)
    ang = 2.0 * np.pi * ((n[:, None] * n[None, :]) % FGROUP_DIM) / FGROUP_DIM
    cc = jnp.asarray(np.cos(ang), F32)
    sc = jnp.asarray(np.sin(ang), F32)
    return pl.pallas_call(
        _wcs_kernel,
        grid=(N_FGROUPS,),
        in_specs=[
            pl.BlockSpec((FGROUP_DIM, FGROUP_DIM), lambda g: (0, 0)),
            pl.BlockSpec((FGROUP_DIM, FGROUP_DIM), lambda g: (0, 0)),
            pl.BlockSpec((1, FGROUP_DIM, FGROUP_DIM), lambda g: (g, 0, 0)),
        ],
        out_specs=pl.BlockSpec((1, FGROUP_DIM, 2 * FGROUP_DIM), lambda g: (g, 0, 0)),
        out_shape=jax.ShapeDtypeStruct((N_FGROUPS, FGROUP_DIM, 2 * FGROUP_DIM), BF16),
        compiler_params=_cparams(("parallel",)),
        name="fold_channel_dft",
    )(cc, sc, w_mix)


DFT_TILE = 256
DFT_TILE_PAD = 16


def _dft_tile(t_len):
    return min(DFT_TILE, t_len // 2)


def _dft_table_kernel(ca_ref, sa_ref, cb_ref, sb_ref, c_ref, s_ref):
    ca = ca_ref[0]
    sa = sa_ref[0]
    cb = cb_ref[...]
    sb = sb_ref[...]
    c_ref[0] = (ca * cb - sa * sb).astype(c_ref.dtype)
    s_ref[0] = (sa * cb + ca * sb).astype(s_ref.dtype)


def _angles(p, t, period):
    return (2.0 * math.pi / period) * ((p[:, None] * t[None, :]) % period).astype(F32)


def _dft_tables(t_len):
    half = t_len // 2
    tm = _dft_tile(t_len)
    rows = tm + DFT_TILE_PAD
    nti = half // tm
    tt = min(2048, half)
    t = jnp.arange(half, dtype=jnp.int32)
    ang_a = _angles(jnp.arange(nti, dtype=jnp.int32) * tm, t, t_len)
    ang_b = _angles(jnp.arange(rows, dtype=jnp.int32), t, t_len)
    ca = jnp.cos(ang_a).reshape(nti, 1, half)
    sa = jnp.sin(ang_a).reshape(nti, 1, half)
    cb = jnp.cos(ang_b)
    sb = jnp.sin(ang_b)
    return pl.pallas_call(
        _dft_table_kernel,
        grid=(half // tt, nti),
        in_specs=[
            pl.BlockSpec((1, 1, tt), lambda j, i: (i, 0, j)),
            pl.BlockSpec((1, 1, tt), lambda j, i: (i, 0, j)),
            pl.BlockSpec((rows, tt), lambda j, i: (0, j)),
            pl.BlockSpec((rows, tt), lambda j, i: (0, j)),
        ],
        out_specs=[
            pl.BlockSpec((1, rows, tt), lambda j, i: (i, 0, j)),
            pl.BlockSpec((1, rows, tt), lambda j, i: (i, 0, j)),
        ],
        out_shape=[jax.ShapeDtypeStruct((nti, rows, half), BF16)] * 2,
        compiler_params=_cparams(("parallel", "parallel")),
        name="dft_tables",
    )(ca, sa, cb, sb)


def _mirror_select(n_out, n_in, offset):
    i = lax.broadcasted_iota(jnp.int32, (n_out, n_in), 0)
    c = lax.broadcasted_iota(jnp.int32, (n_out, n_in), 1)
    return (c == offset - i).astype(BF16)


def _dft_fold_kernel(a_ref, m_ref, x_ref, o_ref):
    tf = a_ref.shape[0]
    a = a_ref[...].astype(F32)
    src = jnp.concatenate([m_ref[...], x_ref[...]], axis=0)
    mir = _dot(_mirror_select(tf, tf + DFT_TILE_PAD, tf), src)
    t_is_zero = (lax.broadcasted_iota(jnp.int32, (tf, 1), 0) == 0) & (pl.program_id(1) == 0)
    o_ref[:, :D_MODEL] = jnp.where(t_is_zero, a[:, :D_MODEL], a[:, :D_MODEL] + mir[:, :D_MODEL]).astype(o_ref.dtype)
    o_ref[:, D_MODEL:] = (a[:, D_MODEL:] - mir[:, D_MODEL:]).astype(o_ref.dtype)


def _dft_fold(p, t_len):
    rows = p.shape[0]
    nb = rows // t_len
    half = t_len // 2
    tf = _dft_tile(t_len)
    ntf = t_len // tf
    pad = DFT_TILE_PAD
    width = p.shape[1]
    return pl.pallas_call(
        _dft_fold_kernel,
        grid=(nb, half // tf),
        in_specs=[
            pl.BlockSpec((tf, width), lambda b, j: (b * ntf + j, 0)),
            pl.BlockSpec((tf, width), lambda b, j: (b * ntf + ntf - 1 - j, 0)),
            pl.BlockSpec((pad, width), lambda b, j: (b * (t_len // pad) + jnp.where(j == 0, 0, (t_len - j * tf) // pad), 0)),
        ],
        out_specs=pl.BlockSpec((tf, width), lambda b, j: (b * (half // tf) + j, 0)),
        out_shape=jax.ShapeDtypeStruct((nb * half, width), BF16),
        compiler_params=_cparams(("parallel", "parallel")),
        name="dft_fold",
    )(p, p, p)


def _fourier_in_kernel(x_ref, g_ref, sc_ref, sh_ref, win_ref, wcs_ref, p_ref, sz_ref):
    h = _rmsnorm(x_ref[...], g_ref[...]) * (1.0 + sc_ref[0]) + sh_ref[0]
    uz = _dot(h.astype(BF16), win_ref[...])
    z = uz[:, D_MODEL:]
    sz_ref[...] = (z * jax.nn.sigmoid(z)).astype(sz_ref.dtype)
    for g in range(N_FGROUPS):
        lo, hi = g * FGROUP_DIM, (g + 1) * FGROUP_DIM
        pg = _dot(uz[:, lo:hi].astype(BF16), wcs_ref[g])
        p_ref[:, lo:hi] = pg[:, :FGROUP_DIM].astype(p_ref.dtype)
        p_ref[:, D_MODEL + lo:D_MODEL + hi] = pg[:, FGROUP_DIM:].astype(p_ref.dtype)


def _fourier_in(x, t_len, g_pre, sc, sh, w_in, wcs):
    rows = x.shape[0]
    tm = min(512, t_len)
    tpb = t_len // tm
    nbm = sc.shape[0]
    modmap = (lambda i: (i // tpb, 0, 0)) if nbm > 1 else (lambda i: (0, 0, 0))
    return pl.pallas_call(
        _fourier_in_kernel,
        grid=(rows // tm,),
        in_specs=[
            pl.BlockSpec((tm, D_MODEL), lambda i: (i, 0)),
            pl.BlockSpec((1, D_MODEL), lambda i: (0, 0)),
            pl.BlockSpec((1, 1, D_MODEL), modmap),
            pl.BlockSpec((1, 1, D_MODEL), modmap),
            pl.BlockSpec((D_MODEL, 2 * D_MODEL), lambda i: (0, 0)),
            pl.BlockSpec((N_FGROUPS, FGROUP_DIM, 2 * FGROUP_DIM), lambda i: (0, 0, 0)),
        ],
        out_specs=[
            pl.BlockSpec((tm, 2 * D_MODEL), lambda i: (i, 0)),
            pl.BlockSpec((tm, D_MODEL), lambda i: (i, 0)),
        ],
        out_shape=[
            jax.ShapeDtypeStruct((rows, 2 * D_MODEL), BF16),
            jax.ShapeDtypeStruct((rows, D_MODEL), BF16),
        ],
        compiler_params=_cparams(("parallel",)),
        name="fourier_in",
    )(x, g_pre, sc, sh, w_in, wcs)


def _fourier_out_kernel(c_ref, s_ref, ec_ref, os_ref, ph_ref, szd_ref, szm_ref, bmix_ref, wout_ref, xd_ref, xm_ref,
                        gpost_ref, gt_ref, y_ref, accc_ref, accs_ref, outd_ref, outm_ref, sem, *, scale, tiles_per_seq):
    b, i, k = pl.program_id(0), pl.program_id(1), pl.program_id(2)
    nb, nti = pl.num_programs(0), pl.num_programs(1)
    tm = outd_ref.shape[0]

    @pl.when(k == 0)
    def _():
        accc_ref[...] = jnp.zeros_like(accc_ref)
        accs_ref[...] = jnp.zeros_like(accs_ref)

    tk = c_ref.shape[2]
    kt = pl.ds(pl.multiple_of(k * tk, tk), tk)
    accc_ref[...] += _dot(c_ref[0], ec_ref[kt, :])
    accs_ref[...] += _dot(s_ref[0], os_ref[kt, :])

    def out_copies(bb, ii):
        direct = pl.multiple_of((bb * tiles_per_seq + ii) * tm, tm)
        mirror = pl.multiple_of((bb * tiles_per_seq + tiles_per_seq - 1 - ii) * tm, tm)
        return (pltpu.make_async_copy(outd_ref, y_ref.at[pl.ds(direct, tm)], sem.at[0]),
                pltpu.make_async_copy(outm_ref, y_ref.at[pl.ds(mirror, tm)], sem.at[1]))

    def finish(f, sz_ref, x_ref, o_ref):
        gated = ((f * scale + bmix_ref[...]) * sz_ref[...].astype(F32)).astype(BF16)
        o = _dot(gated, wout_ref[...])
        o_ref[...] = x_ref[...] + gt_ref[0] * _rmsnorm(o, gpost_ref[...])

    @pl.when(k == pl.num_programs(2) - 1)
    def _():
        @pl.when((b > 0) | (i > 0))
        def _():
            for cp in out_copies(jnp.where(i == 0, b - 1, b), jnp.where(i == 0, nti - 1, i - 1)):
                cp.wait()

        rows = accc_ref.shape[0]
        p = i * tm + lax.broadcasted_iota(jnp.int32, (rows, 1), 0)
        sign = (1 - 2 * (p & 1)).astype(F32)
        fc = accc_ref[...] + sign * ph_ref[0:1, :].astype(F32)
        fs = accs_ref[...]
        finish((fc - fs)[:tm], szd_ref, xd_ref, outd_ref)
        hi, lo = _split_bf16(fc + fs)
        sel = _mirror_select(tm, rows, tm)
        finish(_dot(sel, hi) + _dot(sel, lo), szm_ref, xm_ref, outm_ref)
        for cp in out_copies(b, i):
            cp.start()

        @pl.when((b == nb - 1) & (i == nti - 1))
        def _():
            for cp in out_copies(b, i):
                cp.wait()


def _fourier_out(c_tab, s_tab, eo, p, sz, b_mix, w_out, x, g_post, gt, t_len):
    rows = x.shape[0]
    nb = rows // t_len
    half = t_len // 2
    tm = _dft_tile(t_len)
    tk = min(1024, half)
    pad = DFT_TILE_PAD
    nti, ntk, ntf = half // tm, half // tk, t_len // tm
    nbm = gt.shape[0]
    modmap = (lambda b, i, k: (b, 0, 0)) if nbm > 1 else (lambda b, i, k: (0, 0, 0))
    direct = lambda b, i, k: (b * ntf + i, 0)
    mirror = lambda b, i, k: (b * ntf + ntf - 1 - i, 0)
    const = lambda b, i, k: (0, 0)
    rowspec = lambda m: pl.BlockSpec((tm, D_MODEL), m)
    scale = 1.0 / math.sqrt(t_len * FGROUP_DIM)
    return pl.pallas_call(
        functools.partial(_fourier_out_kernel, scale=scale, tiles_per_seq=ntf),
        grid=(nb, nti, ntk),
        in_specs=[
            pl.BlockSpec((1, tm + pad, tk), lambda b, i, k: (i, 0, k)),
            pl.BlockSpec((1, tm + pad, tk), lambda b, i, k: (i, 0, k)),
            pl.BlockSpec((half, D_MODEL), lambda b, i, k: (b, 0), pipeline_mode=pl.Buffered(1)),
            pl.BlockSpec((half, D_MODEL), lambda b, i, k: (b, 1), pipeline_mode=pl.Buffered(1)),
            pl.BlockSpec((pad, D_MODEL), lambda b, i, k: ((b * t_len + half) // pad, 0)),
            rowspec(direct), rowspec(mirror),
            pl.BlockSpec((1, D_MODEL), const),
            pl.BlockSpec((D_MODEL, D_MODEL), const),
            rowspec(direct), rowspec(mirror),
            pl.BlockSpec((1, D_MODEL), const),
            pl.BlockSpec((1, 1, D_MODEL), modmap),
        ],
        out_specs=pl.BlockSpec(memory_space=pl.ANY),
        out_shape=jax.ShapeDtypeStruct((rows, D_MODEL), F32),
        scratch_shapes=[pltpu.VMEM((tm + pad, D_MODEL), F32)] * 2 + [pltpu.VMEM((tm, D_MODEL), F32)] * 2
        + [pltpu.SemaphoreType.DMA((2,))],
        compiler_params=_cparams(("arbitrary", "arbitrary", "arbitrary")),
        name="fourier_out",
    )(c_tab, s_tab, eo, eo, p, sz, sz, b_mix, w_out, x, x, g_post, gt)


def _rwkv_feat_kernel(*refs, grid_shift, tiles_per_seq, has_vres):
    if grid_shift:
        x_ref, xp_ref, xn_ref = refs[:3]
        refs = refs[3:]
    else:
        x_ref = refs[0]
        refs = refs[1:]
    (g_ref, sc_ref, sh_ref, mu_ref, w4_ref, w1_ref, w2_ref, w0_ref, a1_ref, a2_ref, a0_ref,
     kk_ref, ka_ref, rk_ref, e_ref, et_ref) = refs[:16]
    refs = refs[16:]
    if has_vres:
        vf_ref, v1_ref, v2_ref, v0_ref = refs[:4]
        refs = refs[4:]
    (r_out, v_out, kkn_out, k0_out, k1_out, lw0_out, lw1_out, b0_out, b1_out, g_out, bonus_out) = refs

    g = g_ref[...]
    sc1 = 1.0 + sc_ref[0]
    sh = sh_ref[0]
    tm = x_ref.shape[0]
    h = _rmsnorm(x_ref[...], g) * sc1 + sh
    row = lax.broadcasted_iota(jnp.int32, (tm, 1), 0)
    if grid_shift:
        i = pl.program_id(0)
        first = (i % tiles_per_seq) == 0
        last = (i % tiles_per_seq) == tiles_per_seq - 1
        hp = _rmsnorm(xp_ref[...], g) * sc1 + sh
        hn = _rmsnorm(xn_ref[...], g) * sc1 + sh
        hp = jnp.where(first, 0.0, hp)
        hn = jnp.where(last, 0.0, hn)
        hall = jnp.concatenate([hp, h, hn], axis=0)
        tot = tm + 2 * GRID_W
        up = hall[0:tm]
        down = hall[2 * GRID_W:2 * GRID_W + tm]
        left = pltpu.roll(hall, 1, axis=0)[GRID_W:GRID_W + tm]
        right = pltpu.roll(hall, tot - 1, axis=0)[GRID_W:GRID_W + tm]
        col = row % GRID_W
        left = jnp.where(col == 0, 0.0, left)
        right = jnp.where(col == GRID_W - 1, 0.0, right)
        nbr = (up + down + left + right) * 0.25
    else:
        left = jnp.where(row == 0, 0.0, pltpu.roll(h, 1, axis=0))
        right = jnp.where(row == tm - 1, 0.0, pltpu.roll(h, tm - 1, axis=0))
        nbr = (left + right) * 0.5
    dlt = nbr - h

    def mix(p):
        return h + dlt * mu_ref[p:p + 1, :]

    e = e_ref[...]
    et = et_ref[...]
    tw = jnp.tanh(_dot(mix(4).astype(BF16), w1_ref[...])).astype(BF16)
    ta = _dot(mix(5).astype(BF16), a1_ref[...]).astype(BF16)
    decay_gain = -math.exp(-0.5)
    a_gate = []
    for n, lw_out in enumerate((lw0_out, lw1_out)):
        wz = w0_ref[n:n + 1, :] + _dot(tw[:, n * LORA_DECAY:(n + 1) * LORA_DECAY], w2_ref[n])
        lw_out[...] = decay_gain * jax.nn.sigmoid(wz)
        az = a0_ref[n:n + 1, :] + _dot(ta[:, n * LORA_ICLR:(n + 1) * LORA_ICLR], a2_ref[n])
        a_gate.append(jax.nn.sigmoid(az))

    gz = _dot(mix(3).astype(BF16), w4_ref[3])
    g_out[...] = (gz * jax.nn.sigmoid(gz)).astype(g_out.dtype)

    k = _dot(mix(1).astype(BF16), w4_ref[1])
    kk = k * kk_ref[...]
    n2 = _seg_sum(kk * kk, e)
    inv = 1.0 / jnp.maximum(jnp.sqrt(n2), 1e-12)
    kkn = kk * _seg_bcast(inv, et, split=True)
    kkn_out[...] = kkn
    kd = []
    for a, k_out, b_out in zip(a_gate, (k0_out, k1_out), (b0_out, b1_out)):
        kd.append(k * (1.0 + (a - 1.0) * ka_ref[...]))
        k_out[...] = kd[-1]
        b_out[...] = kkn * a

    xv = mix(2).astype(BF16)
    v = _dot(xv, w4_ref[2])
    if has_vres:
        vz = v0_ref[...] + _dot(_dot(xv, v1_ref[...]).astype(BF16), v2_ref[...])
        v = v + (vf_ref[...] - v) * jax.nn.sigmoid(vz)
    v_out[...] = v

    r = _dot(mix(0).astype(BF16), w4_ref[0])
    r_out[...] = r
    rsum = r * (kd[0] + kd[1]) * rk_ref[...]
    bonus_out[...] = _seg_bcast(_seg_sum(rsum, e), et) * v


def _rwkv_features(x, t_len, grid_shift, g_pre, sc, sh, wts, v_first):
    rows = x.shape[0]
    tm = 256
    tps = t_len // tm
    assert grid_shift or tps == 1
    nbm = sc.shape[0]
    modmap = (lambda i: (i // tps, 0, 0)) if nbm > 1 else (lambda i: (0, 0, 0))
    const2 = lambda i: (0, 0)
    const3 = lambda i: (0, 0, 0)
    rowmap = lambda i: (i, 0)
    has_vres = v_first is not None
    hb = tm // GRID_W
    nhalo = rows // GRID_W

    args, specs = [x], [pl.BlockSpec((tm, D_MODEL), rowmap)]
    if grid_shift:
        args += [x, x]
        specs += [
            pl.BlockSpec((GRID_W, D_MODEL), lambda i: (jnp.maximum(i * hb - 1, 0), 0)),
            pl.BlockSpec((GRID_W, D_MODEL), lambda i: (jnp.minimum((i + 1) * hb, nhalo - 1), 0)),
        ]
    args += [g_pre, sc, sh, wts["mu"], wts["w4"], wts["w1"], wts["w2"], wts["w0"], wts["a1"], wts["a2"], wts["a0"],
             wts["k_k"], wts["k_a"], wts["r_k"], wts["e"], wts["et"]]
    specs += [
        pl.BlockSpec((1, D_MODEL), const2),
        pl.BlockSpec((1, 1, D_MODEL), modmap),
        pl.BlockSpec((1, 1, D_MODEL), modmap),
        pl.BlockSpec((6, D_MODEL), const2),
        pl.BlockSpec((4, D_MODEL, D_MODEL), const3),
        pl.BlockSpec((D_MODEL, 2 * LORA_DECAY), const2),
        pl.BlockSpec((2, LORA_DECAY, D_MODEL), const3),
        pl.BlockSpec((2, D_MODEL), const2),
        pl.BlockSpec((D_MODEL, 2 * LORA_ICLR), const2),
        pl.BlockSpec((2, LORA_ICLR, D_MODEL), const3),
        pl.BlockSpec((2, D_MODEL), const2),
        pl.BlockSpec((1, D_MODEL), const2),
        pl.BlockSpec((1, D_MODEL), const2),
        pl.BlockSpec((1, D_MODEL), const2),
        pl.BlockSpec((D_MODEL, LANES), const2),
        pl.BlockSpec((LANES, D_MODEL), const2),
    ]
    if has_vres:
        args += [v_first, wts["v1"], wts["v2"], wts["v0"]]
        specs += [
            pl.BlockSpec((tm, D_MODEL), rowmap),
            pl.BlockSpec((D_MODEL, LORA_VRES), const2),
            pl.BlockSpec((LORA_VRES, D_MODEL), const2),
            pl.BlockSpec((1, D_MODEL), const2),
        ]
    n_out = 11
    out_dtypes = [F32] * 9 + [BF16, F32]
    return pl.pallas_call(
        functools.partial(_rwkv_feat_kernel, grid_shift=grid_shift, tiles_per_seq=tps, has_vres=has_vres),
        grid=(rows // tm,),
        in_specs=specs,
        out_specs=[pl.BlockSpec((tm, D_MODEL), rowmap)] * n_out,
        out_shape=[jax.ShapeDtypeStruct((rows, D_MODEL), dt) for dt in out_dtypes],
        compiler_params=_cparams(("parallel",)),
        name="rwkv_features",
    )(*args)


NN = (((1,), (0,)), ((), ()))
NT = (((1,), (1,)), ((), ()))
TN = (((0,), (0,)), ((), ()))
ONE_PASS = (1, 1)
LOG_DECAY_PASSES = (1, 2)


def _parts(x, n):
    out = []
    rem = x
    for i in range(n):
        p = rem.astype(BF16)
        out.append(p)
        if i + 1 < n:
            rem = rem - p.astype(F32)
    return tuple(out)


def _mm(a, b, dn, passes=ONE_PASS):
    na, nb = passes
    ap, bp = _parts(a, na), _parts(b, nb)
    lim = max(na, nb)
    acc = None
    for i, x in enumerate(ap):
        for j, y in enumerate(bp):
            if i + j < lim:
                t = lax.dot_general(x, y, dn, preferred_element_type=F32)
                acc = t if acc is None else acc + t
    return acc


def _stack_heads(x):
    first_head = lax.broadcasted_iota(jnp.int32, x.shape, 1) < HEAD_DIM
    zero = jnp.zeros_like(x)
    return jnp.concatenate([jnp.where(first_head, x, zero), jnp.where(first_head, zero, x)], axis=0).astype(BF16)


def _heads_to_rows(x):
    return jnp.concatenate([x[:, :HEAD_DIM], x[:, HEAD_DIM:]], axis=0)


def _wkv_pre_units(units, rev):
    cl = CHUNK
    nu = range(len(units))
    ti = lax.broadcasted_iota(jnp.int32, (cl, cl), 0)
    si = lax.broadcasted_iota(jnp.int32, (cl, cl), 1)
    tri = ((si >= ti) if rev else (si <= ti)).astype(BF16)
    rt = lax.broadcasted_iota(jnp.int32, (cl, LANES), 0)
    ct = lax.broadcasted_iota(jnp.int32, (cl, LANES), 1) % cl
    eye = (rt == ct).astype(F32)
    strict = (ct > rt) if rev else (ct < rt)
    incl = (ct >= rt) if rev else (ct <= rt)
    stack = _stack_heads

    cum = [_mm(tri, u[5], NN, LOG_DECAY_PASSES) for u in units]
    tot = [jnp.sum(u[5], axis=0, keepdims=True) for u in units]
    a_t = [(-units[i][3] * jnp.exp(cum[i] - units[i][5])).astype(BF16) for i in nu]
    r_t = [(units[i][0] * jnp.exp(cum[i])).astype(BF16) for i in nu]
    e_neg = [jnp.exp(-cum[i]) for i in nu]
    bk_h = [jnp.concatenate([stack(units[i][4] * e_neg[i]), stack(units[i][1] * e_neg[i])], axis=0) for i in nu]
    e_rest = [jnp.exp(tot[i] - cum[i]) for i in nu]
    b_r = [units[i][4] * e_rest[i] for i in nu]
    k_r = [units[i][1] * e_rest[i] for i in nu]
    v_s = [stack(u[2]) for u in units]

    scores = [_mm(jnp.concatenate([a_t[i], r_t[i]], axis=0), bk_h[i], NT) for i in nu]
    a_ab = [jnp.where(strict, sc[:cl, :LANES], 0.0) for sc in scores]
    a_ak = [jnp.where(strict, sc[:cl, LANES:], 0.0).astype(BF16) for sc in scores]
    a_r = [jnp.concatenate([jnp.where(incl, sc[cl:, :LANES], 0.0), jnp.where(incl, sc[cl:, LANES:], 0.0)],
                           axis=1).astype(BF16) for sc in scores]

    def same_block(s):
        return (rt // s) == (ct // s)

    inv = [eye + jnp.where(same_block(2), x, 0.0) for x in a_ab]
    s = 2
    while s < cl:
        off = same_block(2 * s) & jnp.logical_not(same_block(s))
        xs = [_mm(jnp.where(off, a_ab[i], 0.0), stack(inv[i]), NN) for i in nu]
        inv = [inv[i] + _mm(inv[i], stack(xs[i]), NN) for i in nu]
        s *= 2

    x2 = [_mm(a_ak[i], v_s[i], NN) for i in nu]
    tu = [_mm(inv[i], jnp.concatenate([stack(a_t[i]), stack(x2[i])], axis=1), NN) for i in nu]
    uh = [t[:, :LANES] for t in tu]
    u0 = [t[:, LANES:] for t in tu]
    uh_s = [stack(x) for x in uh]
    u0v_s = [jnp.concatenate([stack(u0[i]), v_s[i]], axis=0) for i in nu]
    m = [eye * jnp.exp(tot[i]) + _mm(_heads_to_rows(b_r[i]), uh_s[i], TN) for i in nu]
    g = [_mm(jnp.concatenate([_heads_to_rows(b_r[i]), _heads_to_rows(k_r[i])], axis=0), u0v_s[i], TN) for i in nu]
    q = [r_t[i].astype(F32) + _mm(a_r[i][:, :LANES], uh_s[i], NN) for i in nu]
    y0 = [_mm(a_r[i], u0v_s[i], NN) for i in nu]
    return [(m[i], g[i], q[i], y0[i]) for i in nu]


def _wkv_pre_kernel(rc, kc, vc, ac, bc, wc, rl, kl, vl, al, bl, wl, m_ref, g_ref, q_ref, y0_ref, *, rev, cpb, ppb):
    j = pl.program_id(2)

    def run(src):
        ids = [(p, c) for p in range(ppb) for c in range(cpb)]
        units = [tuple(ref[0, pl.ds(c * CHUNK, CHUNK), pl.ds(p * LANES, LANES)] for ref in src) for p, c in ids]
        for (p, c), (m, g, q, y0) in zip(ids, _wkv_pre_units(units, rev)):
            m_ref[0, p, c] = m.astype(m_ref.dtype)
            g_ref[0, p, c] = g.astype(g_ref.dtype)
            q_ref[0, p, c] = q.astype(q_ref.dtype)
            y0_ref[0, p, c] = y0.astype(y0_ref.dtype)

    @pl.when(j == 0)
    def _():
        run((rc, kc, vc, ac, bc, wc))

    @pl.when(j > 0)
    def _():
        run((rl, kl, vl, al, bl, wl))


WKV_PAIRS_PER_BLOCK = 8


def _wkv_pre(ctx_ops, lat_ops, nb, t_ctx, t_lat, rev):
    bt = t_ctx
    cpb = bt // CHUNK
    ppb = WKV_PAIRS_PER_BLOCK
    nlb = t_lat // bt
    nch = (t_ctx + t_lat) // CHUNK
    latmap = lambda b, p, j: (b, jnp.maximum(j - 1, 0), p)
    ctxmap = lambda b, p, j: (b, 0, p)
    outmap = lambda b, p, j: (b, p, j, 0, 0)
    blk = (1, bt, ppb * LANES)
    return pl.pallas_call(
        functools.partial(_wkv_pre_kernel, rev=rev, cpb=cpb, ppb=ppb),
        grid=(nb, N_PAIRS // ppb, nlb + 1),
        in_specs=[pl.BlockSpec(blk, ctxmap)] * 6 + [pl.BlockSpec(blk, latmap)] * 6,
        out_specs=[pl.BlockSpec((1, ppb, cpb, CHUNK, LANES), outmap)] * 4,
        out_shape=[jax.ShapeDtypeStruct((nb, N_PAIRS, nch, CHUNK, LANES), BF16)] * 4,
        compiler_params=_cparams(("parallel", "parallel", "parallel")),
        name="wkv_pre_bwd" if rev else "wkv_pre_fwd",
    )(*ctx_ops, *lat_ops)


def _wkv_scan_kernel(m_ref, g_ref, q_ref, y0_ref, yc_ref, yl_ref, s_ref, *, rev, cpb, nb):
    j = pl.program_id(0)

    @pl.when(j == 0)
    def _():
        s_ref[...] = jnp.zeros_like(s_ref)

    def run(y_ref):
        order = range(cpb - 1, -1, -1) if rev else range(cpb)
        for c in order:
            for b in range(nb):
                for p in range(N_PAIRS):
                    n = b * N_PAIRS + p
                    s = s_ref[n]
                    s_hi = s.astype(BF16)
                    s_lo = (s - s_hi.astype(F32)).astype(BF16)
                    qm = jnp.concatenate([q_ref[b, p, c], m_ref[b, p, c]], axis=0)
                    out = _mm(qm, _stack_heads(s_hi), NN) + _mm(qm, _stack_heads(s_lo), NN)
                    y_ref[b, c * CHUNK:(c + 1) * CHUNK, p * LANES:(p + 1) * LANES] = (
                        out[:CHUNK] + y0_ref[b, p, c].astype(F32))
                    s_ref[n] = out[CHUNK:] + g_ref[b, p, c].astype(F32)

    @pl.when(j == 0)
    def _():
        run(yc_ref)

    @pl.when(j > 0)
    def _():
        run(yl_ref)


def _wkv_scan(m, g, q, y0, nb, t_ctx, t_lat, rev):
    bt = t_ctx
    cpb = bt // CHUNK
    nlb = t_lat // bt
    if rev:
        blkmap = lambda j: (0, 0, jnp.where(j == 0, 0, nlb + 1 - j), 0, 0)
        latmap = lambda j: (0, jnp.where(j == 0, nlb - 1, nlb - j), 0)
    else:
        blkmap = lambda j: (0, 0, j, 0, 0)
        latmap = lambda j: (0, jnp.maximum(j - 1, 0), 0)
    return pl.pallas_call(
        functools.partial(_wkv_scan_kernel, rev=rev, cpb=cpb, nb=nb),
        grid=(nlb + 1,),
        in_specs=[pl.BlockSpec((nb, N_PAIRS, cpb, CHUNK, LANES), blkmap)] * 4,
        out_specs=[pl.BlockSpec((nb, bt, D_MODEL), lambda j: (0, 0, 0)), pl.BlockSpec((nb, bt, D_MODEL), latmap)],
        out_shape=[jax.ShapeDtypeStruct((nb, t_ctx, D_MODEL), F32), jax.ShapeDtypeStruct((nb, t_lat, D_MODEL), F32)],
        scratch_shapes=[pltpu.VMEM((nb * N_PAIRS, HEAD_DIM, LANES), F32)],
        compiler_params=_cparams(("arbitrary",)),
        name="wkv_scan_bwd" if rev else "wkv_scan_fwd",
    )(m, g, q, y0)


def _wkv(ctx_ops, lat_ops, nb, t_ctx, t_lat, rev):
    pre = _wkv_pre(ctx_ops, lat_ops, nb, t_ctx, t_lat, rev)
    return _wkv_scan(*pre, nb, t_ctx, t_lat, rev)


def _rwkv_out_kernel(yf_ref, yb_ref, bonus_ref, g_ref, lnw_ref, lnb_ref, e_ref, et_ref, wout_ref, x_ref, gpost_ref,
                     gt_ref, o_ref):
    e = e_ref[...]
    et = et_ref[...]
    y = yf_ref[...] + yb_ref[...]
    y_hi, y_lo = _split_bf16(y)
    mean = _seg_bcast((_dot(y_hi, e) + _dot(y_lo, e)) * (1.0 / HEAD_DIM), et, split=True)
    d = y - mean
    var = _seg_sum(d * d, e) * (1.0 / HEAD_DIM)
    rstd = _seg_bcast(lax.rsqrt(var + GN_EPS), et)
    yn = d * rstd * lnw_ref[...] + lnb_ref[...]
    t = ((yn + bonus_ref[...]) * g_ref[...].astype(F32)).astype(BF16)
    o = _dot(t, wout_ref[...])
    o_ref[...] = x_ref[...] + gt_ref[0] * _rmsnorm(o, gpost_ref[...])


def _rwkv_out(yf, yb, bonus, g, ln_w, ln_b, e, et, w_out, x, g_post, gt, t_len):
    rows = x.shape[0]
    tm = min(256, t_len)
    tps = t_len // tm
    nbm = gt.shape[0]
    modmap = (lambda i: (i // tps, 0, 0)) if nbm > 1 else (lambda i: (0, 0, 0))
    const2 = lambda i: (0, 0)
    rowmap = lambda i: (i, 0)
    rowspec = pl.BlockSpec((tm, D_MODEL), rowmap)
    vec = pl.BlockSpec((1, D_MODEL), const2)
    return pl.pallas_call(
        _rwkv_out_kernel,
        grid=(rows // tm,),
        in_specs=[rowspec, rowspec, rowspec, rowspec, vec, vec,
                  pl.BlockSpec((D_MODEL, LANES), const2), pl.BlockSpec((LANES, D_MODEL), const2),
                  pl.BlockSpec((D_MODEL, D_MODEL), const2), rowspec, vec, pl.BlockSpec((1, 1, D_MODEL), modmap)],
        out_specs=rowspec,
        out_shape=jax.ShapeDtypeStruct((rows, D_MODEL), F32),
        compiler_params=_cparams(("parallel",)),
        name="rwkv_out",
    )(yf, yb, bonus, g, ln_w, ln_b, e, et, w_out, x, g_post, gt)


def kernel(x, c, ctx, c_ctx, mod_w, mod_b, norm_pre, norm_post, f_w_in, f_w_mix, f_b_mix, f_w_out,
           r_mu, r_w_in, r_w0, r_w1, r_w2, r_a0, r_a1, r_a2, r_v0, r_v1, r_v2,
           r_k_k, r_k_a, r_r_k, r_ln_w, r_ln_b, r_w_out):
    nb, t_lat, d = x.shape
    t_ctx = ctx.shape[1]
    assert d == D_MODEL and t_lat % t_ctx == 0 and t_ctx % CHUNK == 0 and t_lat % GRID_W == 0

    cond_rows = 8
    cond = jnp.zeros((cond_rows, d), F32).at[:nb].set(c).at[nb].set(c_ctx)
    mods = _modulation(cond, mod_w, mod_b)

    e, et = _head_indicator()
    tabs_lat = _dft_tables(t_lat)
    tabs_ctx = _dft_tables(t_ctx)

    def fourier_mix(xr, t_len, tabs, sc, sh, gt, g_pre, g_post, w_in, wcs, b_mix, w_out):
        p, sz = _fourier_in(xr, t_len, g_pre, sc, sh, w_in, wcs)
        return _fourier_out(*tabs, _dft_fold(p, t_len), p, sz, b_mix, w_out, xr, g_post, gt, t_len)

    xl = x.reshape(nb * t_lat, d)
    xc = ctx.reshape(nb * t_ctx, d)
    v_first = None
    vec = lambda a: a.reshape(1, d)

    for i in range(DEPTH):
        last = i == DEPTH - 1
        kind, j = i % 2, i // 2
        m = mods[i]
        sh_l, sc_l, gt_l = (m[:nb, q * d:(q + 1) * d].reshape(nb, 1, d) for q in range(3))
        sh_c, sc_c, gt_c = (m[nb:nb + 1, q * d:(q + 1) * d].reshape(1, 1, d) for q in range(3))
        g_pre, g_post = vec(norm_pre[i]), vec(norm_post[i])
        if kind == 0:
            w_in = f_w_in[j].astype(BF16)
            w_out = f_w_out[j].astype(BF16)
            wcs = _fold_channel_dft(f_w_mix[j])
            b_mix = vec(f_b_mix[j])
            xl_new = fourier_mix(xl, t_lat, tabs_lat, sc_l, sh_l, gt_l, g_pre, g_post, w_in, wcs, b_mix, w_out)
            if not last:
                xc = fourier_mix(xc, t_ctx, tabs_ctx, sc_c, sh_c, gt_c, g_pre, g_post, w_in, wcs, b_mix, w_out)
            xl = xl_new
        else:
            wts = dict(
                mu=r_mu[j], w4=r_w_in[j].astype(BF16),
                w1=jnp.concatenate([r_w1[j, 0], r_w1[j, 1]], axis=1).astype(BF16), w2=r_w2[j].astype(BF16), w0=r_w0[j],
                a1=jnp.concatenate([r_a1[j, 0], r_a1[j, 1]], axis=1).astype(BF16), a2=r_a2[j].astype(BF16), a0=r_a0[j],
                k_k=vec(r_k_k[j]), k_a=vec(r_k_a[j]), r_k=vec(r_r_k[j]), e=e, et=et)
            if j > 0:
                wts.update(v1=r_v1[j - 1].astype(BF16), v2=r_v2[j - 1].astype(BF16), v0=vec(r_v0[j - 1]))
            fc = _rwkv_features(xc, t_ctx, False, g_pre, sc_c, sh_c, wts, None if j == 0 else v_first[0])
            fl = _rwkv_features(xl, t_lat, True, g_pre, sc_l, sh_l, wts, None if j == 0 else v_first[1])
            if j == 0:
                v_first = (fc[1], fl[1])
            ys = []
            for n, rev in enumerate((False, True)):
                pick = lambda f, t: tuple(a.reshape(nb, t, d) for a in (f[0], f[3 + n], f[1], f[2], f[7 + n], f[5 + n]))
                ys.append(_wkv(pick(fc, t_ctx), pick(fl, t_lat), nb, t_ctx, t_lat, rev))
            w_out = r_w_out[j].astype(BF16)
            ln_w, ln_b = vec(r_ln_w[j]), vec(r_ln_b[j])
            xl_new = _rwkv_out(ys[0][1].reshape(-1, d), ys[1][1].reshape(-1, d), fl[10], fl[9], ln_w, ln_b, e, et,
                               w_out, xl, g_post, gt_l, t_lat)
            if not last:
                xc = _rwkv_out(ys[0][0].reshape(-1, d), ys[1][0].reshape(-1, d), fc[10], fc[9], ln_w, ln_b, e, et,
                               w_out, xc, g_post, gt_c, t_ctx)
            xl = xl_new
    return xl.reshape(nb, t_lat, d)
```

```python
import functools
import math

import numpy as np
import jax
import jax.numpy as jnp
from jax import lax
from jax.experimental import pallas as pl
from jax.experimental.pallas import tpu as pltpu

F32 = jnp.float32
BF16 = jnp.bfloat16
HIGHEST = lax.Precision.HIGHEST

D_MODEL = 1024
DEPTH = 4
GRID_W = 64
HEAD_DIM = 64
N_HEADS = D_MODEL // HEAD_DIM
N_FGROUPS = 8
FGROUP_DIM = D_MODEL // N_FGROUPS
LORA_DECAY = 64
LORA_ICLR = 64
LORA_VRES = 32
RMS_EPS = 1e-6
GN_EPS = 64e-5

LANES = 128
CHUNK = 64
HEADS_PER_TILE = LANES // HEAD_DIM
N_PAIRS = D_MODEL // LANES
VMEM_LIMIT = 56 * 1024 * 1024


def _cparams(sem):
    return pltpu.CompilerParams(dimension_semantics=sem, vmem_limit_bytes=VMEM_LIMIT)


def _dot(a, b):
    return jnp.dot(a, b, preferred_element_type=F32)


def _dot_hi(a, b):
    return jnp.dot(a, b, preferred_element_type=F32, precision=HIGHEST)


def _rmsnorm(x, g):
    return x * lax.rsqrt(jnp.mean(x * x, axis=-1, keepdims=True) + RMS_EPS) * g


def _split_bf16(x):
    hi = x.astype(BF16)
    lo = (x - hi.astype(F32)).astype(BF16)
    return hi, lo


def _seg_sum(x, e):
    return _dot(x.astype(BF16), e)


def _seg_bcast(s, et, split=False):
    if not split:
        return _dot(s.astype(BF16), et)
    hi, lo = _split_bf16(s)
    return _dot(hi, et) + _dot(lo, et)


def _head_indicator():
    e = np.zeros((D_MODEL, LANES), np.float32)
    e[np.arange(D_MODEL), np.arange(D_MODEL) // HEAD_DIM] = 1.0
    return jnp.asarray(e, BF16), jnp.asarray(e.T, BF16)


def _mod_kernel(cond_ref, w_ref, b_ref, o_ref):
    cond = cond_ref[...]
    s = cond * jax.nn.sigmoid(cond)
    o_ref[0] = _dot_hi(s, w_ref[0]) + b_ref[0]


def _modulation(cond, mod_w, mod_b):
    rows = cond.shape[0]
    tn = 1536
    return pl.pallas_call(
        _mod_kernel,
        grid=(DEPTH, 3 * D_MODEL // tn),
        in_specs=[
            pl.BlockSpec((rows, D_MODEL), lambda i, j: (0, 0)),
            pl.BlockSpec((1, D_MODEL, tn), lambda i, j: (i, 0, j)),
            pl.BlockSpec((1, 1, tn), lambda i, j: (i, 0, j)),
        ],
        out_specs=pl.BlockSpec((1, rows, tn), lambda i, j: (i, 0, j)),
        out_shape=jax.ShapeDtypeStruct((DEPTH, rows, 3 * D_MODEL), F32),
        compiler_params=_cparams(("parallel", "parallel")),
        name="modulation",
    )(cond, mod_w, mod_b.reshape(DEPTH, 1, 3 * D_MODEL))


def _wcs_kernel(cc_ref, sc_ref, w_ref, o_ref):
    w = w_ref[0]
    o_ref[0, :, :FGROUP_DIM] = _dot_hi(cc_ref[...], w).astype(o_ref.dtype)
    o_ref[0, :, FGROUP_DIM:] = _dot_hi(sc_ref[...], w).astype(o_ref.dtype)


def _fold_channel_dft(w_mix):
    n = np.arange(FGROUP_DIM)
    ang = 2.0 * np.pi * ((n[:, None] * n[None, :]) % FGROUP_DIM) / FGROUP_DIM
    cc = jnp.asarray(np.cos(ang), F32)
    sc = jnp.asarray(np.sin(ang), F32)
    return pl.pallas_call(
        _wcs_kernel,
        grid=(N_FGROUPS,),
        in_specs=[
            pl.BlockSpec((FGROUP_DIM, FGROUP_DIM), lambda g: (0, 0)),
            pl.BlockSpec((FGROUP_DIM, FGROUP_DIM), lambda g: (0, 0)),
            pl.BlockSpec((1, FGROUP_DIM, FGROUP_DIM), lambda g: (g, 0, 0)),
        ],
        out_specs=pl.BlockSpec((1, FGROUP_DIM, 2 * FGROUP_DIM), lambda g: (g, 0, 0)),
        out_shape=jax.ShapeDtypeStruct((N_FGROUPS, FGROUP_DIM, 2 * FGROUP_DIM), BF16),
        compiler_params=_cparams(("parallel",)),
        name="fold_channel_dft",
    )(cc, sc, w_mix)


DFT_TILE = 512
DFT_TILE_PAD = 16


def _dft_tile(t_len):
    return min(DFT_TILE, t_len // 2)


def _dft_table_kernel(ca_ref, sa_ref, cb_ref, sb_ref, c_ref, s_ref):
    ca = ca_ref[0]
    sa = sa_ref[0]
    cb = cb_ref[...]
    sb = sb_ref[...]
    c_ref[0] = (ca * cb - sa * sb).astype(c_ref.dtype)
    s_ref[0] = (sa * cb + ca * sb).astype(s_ref.dtype)


def _angles(p, t, period):
    return (2.0 * math.pi / period) * ((p[:, None] * t[None, :]) % period).astype(F32)


def _dft_tables(t_len):
    half = t_len // 2
    tm = _dft_tile(t_len)
    rows = tm + DFT_TILE_PAD
    nti = half // tm
    tt = min(2048, half)
    t = jnp.arange(half, dtype=jnp.int32)
    ang_a = _angles(jnp.arange(nti, dtype=jnp.int32) * tm, t, t_len)
    ang_b = _angles(jnp.arange(rows, dtype=jnp.int32), t, t_len)
    ca = jnp.cos(ang_a).reshape(nti, 1, half)
    sa = jnp.sin(ang_a).reshape(nti, 1, half)
    cb = jnp.cos(ang_b)
    sb = jnp.sin(ang_b)
    return pl.pallas_call(
        _dft_table_kernel,
        grid=(half // tt, nti),
        in_specs=[
            pl.BlockSpec((1, 1, tt), lambda j, i: (i, 0, j)),
            pl.BlockSpec((1, 1, tt), lambda j, i: (i, 0, j)),
            pl.BlockSpec((rows, tt), lambda j, i: (0, j)),
            pl.BlockSpec((rows, tt), lambda j, i: (0, j)),
        ],
        out_specs=[
            pl.BlockSpec((1, rows, tt), lambda j, i: (i, 0, j)),
            pl.BlockSpec((1, rows, tt), lambda j, i: (i, 0, j)),
        ],
        out_shape=[jax.ShapeDtypeStruct((nti, rows, half), BF16)] * 2,
        compiler_params=_cparams(("parallel", "parallel")),
        name="dft_tables",
    )(ca, sa, cb, sb)


def _mirror_select(n_out, n_in, offset):
    i = lax.broadcasted_iota(jnp.int32, (n_out, n_in), 0)
    c = lax.broadcasted_iota(jnp.int32, (n_out, n_in), 1)
    return (c == offset - i).astype(BF16)


def _dft_fold_kernel(a_ref, m_ref, x_ref, o_ref):
    tf = a_ref.shape[0]
    a = a_ref[...].astype(F32)
    src = jnp.concatenate([m_ref[...], x_ref[...]], axis=0)
    mir = _dot(_mirror_select(tf, tf + DFT_TILE_PAD, tf), src)
    t_is_zero = (lax.broadcasted_iota(jnp.int32, (tf, 1), 0) == 0) & (pl.program_id(1) == 0)
    o_ref[:, :D_MODEL] = jnp.where(t_is_zero, a[:, :D_MODEL], a[:, :D_MODEL] + mir[:, :D_MODEL]).astype(o_ref.dtype)
    o_ref[:, D_MODEL:] = (a[:, D_MODEL:] - mir[:, D_MODEL:]).astype(o_ref.dtype)


def _dft_fold(p, t_len):
    rows = p.shape[0]
    nb = rows // t_len
    half = t_len // 2
    tf = _dft_tile(t_len)
    ntf = t_len // tf
    pad = DFT_TILE_PAD
    width = p.shape[1]
    return pl.pallas_call(
        _dft_fold_kernel,
        grid=(nb, half // tf),
        in_specs=[
            pl.BlockSpec((tf, width), lambda b, j: (b * ntf + j, 0)),
            pl.BlockSpec((tf, width), lambda b, j: (b * ntf + ntf - 1 - j, 0)),
            pl.BlockSpec((pad, width), lambda b, j: (b * (t_len // pad) + jnp.where(j == 0, 0, (t_len - j * tf) // pad), 0)),
        ],
        out_specs=pl.BlockSpec((tf, width), lambda b, j: (b * (half // tf) + j, 0)),
        out_shape=jax.ShapeDtypeStruct((nb * half, width), BF16),
        compiler_params=_cparams(("parallel", "parallel")),
        name="dft_fold",
    )(p, p, p)


def _fourier_in_kernel(x_ref, g_ref, sc_ref, sh_ref, win_ref, wcs_ref, p_ref, sz_ref):
    h = _rmsnorm(x_ref[...], g_ref[...]) * (1.0 + sc_ref[0]) + sh_ref[0]
    uz = _dot(h.astype(BF16), win_ref[...])
    z = uz[:, D_MODEL:]
    sz_ref[...] = (z * jax.nn.sigmoid(z)).astype(sz_ref.dtype)
    for g in range(N_FGROUPS):
        lo, hi = g * FGROUP_DIM, (g + 1) * FGROUP_DIM
        pg = _dot(uz[:, lo:hi].astype(BF16), wcs_ref[g])
        p_ref[:, lo:hi] = pg[:, :FGROUP_DIM].astype(p_ref.dtype)
        p_ref[:, D_MODEL + lo:D_MODEL + hi] = pg[:, FGROUP_DIM:].astype(p_ref.dtype)


def _fourier_in(x, t_len, g_pre, sc, sh, w_in, wcs):
    rows = x.shape[0]
    tm = min(512, t_len)
    tpb = t_len // tm
    nbm = sc.shape[0]
    modmap = (lambda i: (i // tpb, 0, 0)) if nbm > 1 else (lambda i: (0, 0, 0))
    return pl.pallas_call(
        _fourier_in_kernel,
        grid=(rows // tm,),
        in_specs=[
            pl.BlockSpec((tm, D_MODEL), lambda i: (i, 0)),
            pl.BlockSpec((1, D_MODEL), lambda i: (0, 0)),
            pl.BlockSpec((1, 1, D_MODEL), modmap),
            pl.BlockSpec((1, 1, D_MODEL), modmap),
            pl.BlockSpec((D_MODEL, 2 * D_MODEL), lambda i: (0, 0)),
            pl.BlockSpec((N_FGROUPS, FGROUP_DIM, 2 * FGROUP_DIM), lambda i: (0, 0, 0)),
        ],
        out_specs=[
            pl.BlockSpec((tm, 2 * D_MODEL), lambda i: (i, 0)),
            pl.BlockSpec((tm, D_MODEL), lambda i: (i, 0)),
        ],
        out_shape=[
            jax.ShapeDtypeStruct((rows, 2 * D_MODEL), BF16),
            jax.ShapeDtypeStruct((rows, D_MODEL), BF16),
        ],
        compiler_params=_cparams(("parallel",)),
        name="fourier_in",
    )(x, g_pre, sc, sh, w_in, wcs)


def _fourier_out_kernel(c_ref, s_ref, ec_ref, os_ref, ph_ref, szd_ref, szm_ref, bmix_ref, wout_ref, xd_ref, xm_ref,
                        gpost_ref, gt_ref, y_ref, accc_ref, accs_ref, outd_ref, outm_ref, sem, *, scale, tiles_per_seq):
    b, i, k = pl.program_id(0), pl.program_id(1), pl.program_id(2)
    nb, nti = pl.num_programs(0), pl.num_programs(1)
    tm = outd_ref.shape[0]

    @pl.when(k == 0)
    def _():
        accc_ref[...] = jnp.zeros_like(accc_ref)
        accs_ref[...] = jnp.zeros_like(accs_ref)

    accc_ref[...] += _dot(c_ref[0], ec_ref[...])
    accs_ref[...] += _dot(s_ref[0], os_ref[...])

    def out_copies(bb, ii):
        direct = pl.multiple_of((bb * tiles_per_seq + ii) * tm, tm)
        mirror = pl.multiple_of((bb * tiles_per_seq + tiles_per_seq - 1 - ii) * tm, tm)
        return (pltpu.make_async_copy(outd_ref, y_ref.at[pl.ds(direct, tm)], sem.at[0]),
                pltpu.make_async_copy(outm_ref, y_ref.at[pl.ds(mirror, tm)], sem.at[1]))

    def finish(f, sz_ref, x_ref, o_ref):
        gated = ((f * scale + bmix_ref[...]) * sz_ref[...].astype(F32)).astype(BF16)
        o = _dot(gated, wout_ref[...])
        o_ref[...] = x_ref[...] + gt_ref[0] * _rmsnorm(o, gpost_ref[...])

    @pl.when(k == pl.num_programs(2) - 1)
    def _():
        @pl.when((b > 0) | (i > 0))
        def _():
            for cp in out_copies(jnp.where(i == 0, b - 1, b), jnp.where(i == 0, nti - 1, i - 1)):
                cp.wait()

        rows = accc_ref.shape[0]
        p = i * tm + lax.broadcasted_iota(jnp.int32, (rows, 1), 0)
        sign = (1 - 2 * (p & 1)).astype(F32)
        fc = accc_ref[...] + sign * ph_ref[0:1, :].astype(F32)
        fs = accs_ref[...]
        finish((fc - fs)[:tm], szd_ref, xd_ref, outd_ref)
        hi, lo = _split_bf16(fc + fs)
        sel = _mirror_select(tm, rows, tm)
        finish(_dot(sel, hi) + _dot(sel, lo), szm_ref, xm_ref, outm_ref)
        for cp in out_copies(b, i):
            cp.start()

        @pl.when((b == nb - 1) & (i == nti - 1))
        def _():
            for cp in out_copies(b, i):
                cp.wait()


def _fourier_out(c_tab, s_tab, eo, p, sz, b_mix, w_out, x, g_post, gt, t_len):
    rows = x.shape[0]
    nb = rows // t_len
    half = t_len // 2
    tm = _dft_tile(t_len)
    tk = min(1024, half)
    pad = DFT_TILE_PAD
    nti, ntk, ntf = half // tm, half // tk, t_len // tm
    nbm = gt.shape[0]
    modmap = (lambda b, i, k: (b, 0, 0)) if nbm > 1 else (lambda b, i, k: (0, 0, 0))
    direct = lambda b, i, k: (b * ntf + i, 0)
    mirror = lambda b, i, k: (b * ntf + ntf - 1 - i, 0)
    const = lambda b, i, k: (0, 0)
    rowspec = lambda m: pl.BlockSpec((tm, D_MODEL), m)
    scale = 1.0 / math.sqrt(t_len * FGROUP_DIM)
    return pl.pallas_call(
        functools.partial(_fourier_out_kernel, scale=scale, tiles_per_seq=ntf),
        grid=(nb, nti, ntk),
        in_specs=[
            pl.BlockSpec((1, tm + pad, tk), lambda b, i, k: (i, 0, k)),
            pl.BlockSpec((1, tm + pad, tk), lambda b, i, k: (i, 0, k)),
            pl.BlockSpec((tk, D_MODEL), lambda b, i, k: (b * ntk + k, 0)),
            pl.BlockSpec((tk, D_MODEL), lambda b, i, k: (b * ntk + k, 1)),
            pl.BlockSpec((pad, D_MODEL), lambda b, i, k: ((b * t_len + half) // pad, 0)),
            rowspec(direct), rowspec(mirror),
            pl.BlockSpec((1, D_MODEL), const),
            pl.BlockSpec((D_MODEL, D_MODEL), const),
            rowspec(direct), rowspec(mirror),
            pl.BlockSpec((1, D_MODEL), const),
            pl.BlockSpec((1, 1, D_MODEL), modmap),
        ],
        out_specs=pl.BlockSpec(memory_space=pl.ANY),
        out_shape=jax.ShapeDtypeStruct((rows, D_MODEL), F32),
        scratch_shapes=[pltpu.VMEM((tm + pad, D_MODEL), F32)] * 2 + [pltpu.VMEM((tm, D_MODEL), F32)] * 2
        + [pltpu.SemaphoreType.DMA((2,))],
        compiler_params=_cparams(("arbitrary", "arbitrary", "arbitrary")),
        name="fourier_out",
    )(c_tab, s_tab, eo, eo, p, sz, sz, b_mix, w_out, x, x, g_post, gt)


def _rwkv_feat_kernel(*refs, grid_shift, tiles_per_seq, has_vres):
    if grid_shift:
        x_ref, xp_ref, xn_ref = refs[:3]
        refs = refs[3:]
    else:
        x_ref = refs[0]
        refs = refs[1:]
    (g_ref, sc_ref, sh_ref, mu_ref, w4_ref, w1_ref, w2_ref, w0_ref, a1_ref, a2_ref, a0_ref,
     kk_ref, ka_ref, rk_ref, e_ref, et_ref) = refs[:16]
    refs = refs[16:]
    if has_vres:
        vf_ref, v1_ref, v2_ref, v0_ref = refs[:4]
        refs = refs[4:]
    (r_out, v_out, kkn_out, k0_out, k1_out, lw0_out, lw1_out, b0_out, b1_out, g_out, bonus_out) = refs

    g = g_ref[...]
    sc1 = 1.0 + sc_ref[0]
    sh = sh_ref[0]
    tm = x_ref.shape[0]
    h = _rmsnorm(x_ref[...], g) * sc1 + sh
    row = lax.broadcasted_iota(jnp.int32, (tm, 1), 0)
    if grid_shift:
        i = pl.program_id(0)
        first = (i % tiles_per_seq) == 0
        last = (i % tiles_per_seq) == tiles_per_seq - 1
        hp = _rmsnorm(xp_ref[...], g) * sc1 + sh
        hn = _rmsnorm(xn_ref[...], g) * sc1 + sh
        hp = jnp.where(first, 0.0, hp)
        hn = jnp.where(last, 0.0, hn)
        hall = jnp.concatenate([hp, h, hn], axis=0)
        tot = tm + 2 * GRID_W
        up = hall[0:tm]
        down = hall[2 * GRID_W:2 * GRID_W + tm]
        left = pltpu.roll(hall, 1, axis=0)[GRID_W:GRID_W + tm]
        right = pltpu.roll(hall, tot - 1, axis=0)[GRID_W:GRID_W + tm]
        col = row % GRID_W
        left = jnp.where(col == 0, 0.0, left)
        right = jnp.where(col == GRID_W - 1, 0.0, right)
        nbr = (up + down + left + right) * 0.25
    else:
        left = jnp.where(row == 0, 0.0, pltpu.roll(h, 1, axis=0))
        right = jnp.where(row == tm - 1, 0.0, pltpu.roll(h, tm - 1, axis=0))
        nbr = (left + right) * 0.5
    dlt = nbr - h

    def mix(p):
        return h + dlt * mu_ref[p:p + 1, :]

    e = e_ref[...]
    et = et_ref[...]
    tw = jnp.tanh(_dot(mix(4).astype(BF16), w1_ref[...])).astype(BF16)
    ta = _dot(mix(5).astype(BF16), a1_ref[...]).astype(BF16)
    decay_gain = -math.exp(-0.5)
    a_gate = []
    for n, lw_out in enumerate((lw0_out, lw1_out)):
        wz = w0_ref[n:n + 1, :] + _dot(tw[:, n * LORA_DECAY:(n + 1) * LORA_DECAY], w2_ref[n])
        lw_out[...] = decay_gain * jax.nn.sigmoid(wz)
        az = a0_ref[n:n + 1, :] + _dot(ta[:, n * LORA_ICLR:(n + 1) * LORA_ICLR], a2_ref[n])
        a_gate.append(jax.nn.sigmoid(az))

    gz = _dot(mix(3).astype(BF16), w4_ref[3])
    g_out[...] = (gz * jax.nn.sigmoid(gz)).astype(g_out.dtype)

    k = _dot(mix(1).astype(BF16), w4_ref[1])
    kk = k * kk_ref[...]
    n2 = _seg_sum(kk * kk, e)
    inv = 1.0 / jnp.maximum(jnp.sqrt(n2), 1e-12)
    kkn = kk * _seg_bcast(inv, et, split=True)
    kkn_out[...] = kkn
    kd = []
    for a, k_out, b_out in zip(a_gate, (k0_out, k1_out), (b0_out, b1_out)):
        kd.append(k * (1.0 + (a - 1.0) * ka_ref[...]))
        k_out[...] = kd[-1]
        b_out[...] = kkn * a

    xv = mix(2).astype(BF16)
    v = _dot(xv, w4_ref[2])
    if has_vres:
        vz = v0_ref[...] + _dot(_dot(xv, v1_ref[...]).astype(BF16), v2_ref[...])
        v = v + (vf_ref[...] - v) * jax.nn.sigmoid(vz)
    v_out[...] = v

    r = _dot(mix(0).astype(BF16), w4_ref[0])
    r_out[...] = r
    rsum = r * (kd[0] + kd[1]) * rk_ref[...]
    bonus_out[...] = (_seg_bcast(_seg_sum(rsum, e), et) * v).astype(bonus_out.dtype)


def _rwkv_features(x, t_len, grid_shift, g_pre, sc, sh, wts, v_first):
    rows = x.shape[0]
    tm = 256
    tps = t_len // tm
    assert grid_shift or tps == 1
    nbm = sc.shape[0]
    modmap = (lambda i: (i // tps, 0, 0)) if nbm > 1 else (lambda i: (0, 0, 0))
    const2 = lambda i: (0, 0)
    const3 = lambda i: (0, 0, 0)
    rowmap = lambda i: (i, 0)
    has_vres = v_first is not None
    hb = tm // GRID_W
    nhalo = rows // GRID_W

    args, specs = [x], [pl.BlockSpec((tm, D_MODEL), rowmap)]
    if grid_shift:
        args += [x, x]
        specs += [
            pl.BlockSpec((GRID_W, D_MODEL), lambda i: (jnp.maximum(i * hb - 1, 0), 0)),
            pl.BlockSpec((GRID_W, D_MODEL), lambda i: (jnp.minimum((i + 1) * hb, nhalo - 1), 0)),
        ]
    args += [g_pre, sc, sh, wts["mu"], wts["w4"], wts["w1"], wts["w2"], wts["w0"], wts["a1"], wts["a2"], wts["a0"],
             wts["k_k"], wts["k_a"], wts["r_k"], wts["e"], wts["et"]]
    specs += [
        pl.BlockSpec((1, D_MODEL), const2),
        pl.BlockSpec((1, 1, D_MODEL), modmap),
        pl.BlockSpec((1, 1, D_MODEL), modmap),
        pl.BlockSpec((6, D_MODEL), const2),
        pl.BlockSpec((4, D_MODEL, D_MODEL), const3),
        pl.BlockSpec((D_MODEL, 2 * LORA_DECAY), const2),
        pl.BlockSpec((2, LORA_DECAY, D_MODEL), const3),
        pl.BlockSpec((2, D_MODEL), const2),
        pl.BlockSpec((D_MODEL, 2 * LORA_ICLR), const2),
        pl.BlockSpec((2, LORA_ICLR, D_MODEL), const3),
        pl.BlockSpec((2, D_MODEL), const2),
        pl.BlockSpec((1, D_MODEL), const2),
        pl.BlockSpec((1, D_MODEL), const2),
        pl.BlockSpec((1, D_MODEL), const2),
        pl.BlockSpec((D_MODEL, LANES), const2),
        pl.BlockSpec((LANES, D_MODEL), const2),
    ]
    if has_vres:
        args += [v_first, wts["v1"], wts["v2"], wts["v0"]]
        specs += [
            pl.BlockSpec((tm, D_MODEL), rowmap),
            pl.BlockSpec((D_MODEL, LORA_VRES), const2),
            pl.BlockSpec((LORA_VRES, D_MODEL), const2),
            pl.BlockSpec((1, D_MODEL), const2),
        ]
    n_out = 11
    out_dtypes = [F32] * 9 + [BF16, BF16]
    return pl.pallas_call(
        functools.partial(_rwkv_feat_kernel, grid_shift=grid_shift, tiles_per_seq=tps, has_vres=has_vres),
        grid=(rows // tm,),
        in_specs=specs,
        out_specs=[pl.BlockSpec((tm, D_MODEL), rowmap)] * n_out,
        out_shape=[jax.ShapeDtypeStruct((rows, D_MODEL), dt) for dt in out_dtypes],
        compiler_params=_cparams(("parallel",)),
        name="rwkv_features",
    )(*args)


NN = (((1,), (0,)), ((), ()))
NT = (((1,), (1,)), ((), ()))


def _mm(a, b, dn):
    return lax.dot_general(a.astype(BF16), b.astype(BF16), dn, preferred_element_type=F32)


def _stack_heads(x):
    first_head = lax.broadcasted_iota(jnp.int32, x.shape, 1) < HEAD_DIM
    zero = jnp.zeros_like(x)
    return jnp.concatenate([jnp.where(first_head, x, zero), jnp.where(first_head, zero, x)], axis=0).astype(BF16)


def _prefix_rows(x, rev):
    n = x.shape[0]
    row = lax.broadcasted_iota(jnp.int32, (n, 1), 0)
    s = 1
    while s < n:
        if rev:
            x = x + jnp.where(row < n - s, pltpu.roll(x, n - s, axis=0), 0.0)
        else:
            x = x + jnp.where(row >= s, pltpu.roll(x, s, axis=0), 0.0)
        s *= 2
    return x


def _heads_to_rows(x):
    return jnp.concatenate([x[:, :HEAD_DIM], x[:, HEAD_DIM:]], axis=0)


def _wkv_pre_units(units, rev):
    cl = CHUNK
    nu = range(len(units))
    rt = lax.broadcasted_iota(jnp.int32, (cl, LANES), 0)
    ct = lax.broadcasted_iota(jnp.int32, (cl, LANES), 1) % cl
    eye = (rt == ct).astype(F32)
    strict = (ct > rt) if rev else (ct < rt)
    incl = (ct >= rt) if rev else (ct <= rt)
    stack = _stack_heads

    cum = [_prefix_rows(u[5], rev) for u in units]
    tot = [jnp.sum(u[5], axis=0, keepdims=True) for u in units]
    a_t = [(-units[i][3] * jnp.exp(cum[i] - units[i][5])).astype(BF16) for i in nu]
    r_t = [(units[i][0] * jnp.exp(cum[i])).astype(BF16) for i in nu]
    e_neg = [jnp.exp(-cum[i]) for i in nu]
    bk_h = [jnp.concatenate([stack(units[i][4] * e_neg[i]), stack(units[i][1] * e_neg[i])], axis=0) for i in nu]
    e_rest = [jnp.exp(tot[i] - cum[i]) for i in nu]
    b_r = [units[i][4] * e_rest[i] for i in nu]
    k_r = [units[i][1] * e_rest[i] for i in nu]
    v_s = [stack(u[2]) for u in units]

    scores = [_mm(jnp.concatenate([a_t[i], r_t[i]], axis=0), bk_h[i], NT) for i in nu]
    a_ab = [jnp.where(strict, sc[:cl, :LANES], 0.0) for sc in scores]
    a_ak = [jnp.where(strict, sc[:cl, LANES:], 0.0).astype(BF16) for sc in scores]
    a_r = [jnp.concatenate([jnp.where(incl, sc[cl:, :LANES], 0.0), jnp.where(incl, sc[cl:, LANES:], 0.0)],
                           axis=1).astype(BF16) for sc in scores]

    def same_block(s):
        return (rt // s) == (ct // s)

    inv = [eye + jnp.where(same_block(2), x, 0.0) for x in a_ab]
    s = 2
    while s < cl:
        off = same_block(2 * s) & jnp.logical_not(same_block(s))
        xs = [_mm(jnp.where(off, a_ab[i], 0.0), stack(inv[i]), NN) for i in nu]
        inv = [inv[i] + _mm(inv[i], stack(xs[i]), NN) for i in nu]
        s *= 2

    x2 = [_mm(a_ak[i], v_s[i], NN) for i in nu]
    tu = [_mm(inv[i], jnp.concatenate([stack(a_t[i]), stack(x2[i])], axis=1), NN) for i in nu]
    uh = [t[:, :LANES] for t in tu]
    u0 = [t[:, LANES:] for t in tu]
    uh_s = [stack(x) for x in uh]
    u0v_s = [jnp.concatenate([stack(u0[i]), v_s[i]], axis=0) for i in nu]
    bk_t = [jnp.concatenate([_heads_to_rows(b_r[i]), _heads_to_rows(k_r[i])], axis=0).T.astype(BF16) for i in nu]
    mq = [_mm(jnp.concatenate([bk_t[i][:, :LANES], a_r[i][:, :LANES]], axis=0), uh_s[i], NN) for i in nu]
    gy = [_mm(jnp.concatenate([bk_t[i], a_r[i]], axis=0), u0v_s[i], NN) for i in nu]
    return [(eye * jnp.exp(tot[i]) + mq[i][:cl], gy[i][:cl], r_t[i].astype(F32) + mq[i][cl:], gy[i][cl:]) for i in nu]


def _wkv_pre_kernel(rc, kc, vc, ac, bc, wc, rl, kl, vl, al, bl, wl, m_ref, g_ref, q_ref, y0_ref, *, rev, cpb, ppb):
    j = pl.program_id(2)

    def run(src):
        ids = [(p, c) for p in range(ppb) for c in range(cpb)]
        units = [tuple(ref[0, pl.ds(c * CHUNK, CHUNK), pl.ds(p * LANES, LANES)] for ref in src) for p, c in ids]
        for (p, c), (m, g, q, y0) in zip(ids, _wkv_pre_units(units, rev)):
            m_ref[0, p, c] = m.astype(m_ref.dtype)
            g_ref[0, p, c] = g.astype(g_ref.dtype)
            q_ref[0, p, c] = q.astype(q_ref.dtype)
            y0_ref[0, p, c] = y0.astype(y0_ref.dtype)

    @pl.when(j == 0)
    def _():
        run((rc, kc, vc, ac, bc, wc))

    @pl.when(j > 0)
    def _():
        run((rl, kl, vl, al, bl, wl))


WKV_PAIRS_PER_BLOCK = 8


def _wkv_pre(ctx_ops, lat_ops, nb, t_ctx, t_lat, rev):
    bt = t_ctx
    cpb = bt // CHUNK
    ppb = WKV_PAIRS_PER_BLOCK
    nlb = t_lat // bt
    nch = (t_ctx + t_lat) // CHUNK
    latmap = lambda b, p, j: (b, jnp.maximum(j - 1, 0), p)
    ctxmap = lambda b, p, j: (b, 0, p)
    outmap = lambda b, p, j: (b, p, j, 0, 0)
    blk = (1, bt, ppb * LANES)
    return pl.pallas_call(
        functools.partial(_wkv_pre_kernel, rev=rev, cpb=cpb, ppb=ppb),
        grid=(nb, N_PAIRS // ppb, nlb + 1),
        in_specs=[pl.BlockSpec(blk, ctxmap)] * 6 + [pl.BlockSpec(blk, latmap)] * 6,
        out_specs=[pl.BlockSpec((1, ppb, cpb, CHUNK, LANES), outmap)] * 4,
        out_shape=[jax.ShapeDtypeStruct((nb, N_PAIRS, nch, CHUNK, LANES), BF16)] * 4,
        compiler_params=_cparams(("parallel", "parallel", "parallel")),
        name="wkv_pre_bwd" if rev else "wkv_pre_fwd",
    )(*ctx_ops, *lat_ops)


def _wkv_scan_kernel(m_ref, g_ref, q_ref, y0_ref, yc_ref, yl_ref, s_ref, *, rev, cpb, nb):
    j = pl.program_id(0)

    @pl.when(j == 0)
    def _():
        s_ref[...] = jnp.zeros_like(s_ref)

    def run(y_ref):
        order = range(cpb - 1, -1, -1) if rev else range(cpb)
        for c in order:
            for b in range(nb):
                for p in range(N_PAIRS):
                    n = b * N_PAIRS + p
                    s = s_ref[n]
                    s_hi = s.astype(BF16)
                    s_lo = (s - s_hi.astype(F32)).astype(BF16)
                    qm = jnp.concatenate([q_ref[b, p, c], m_ref[b, p, c]], axis=0)
                    out = _mm(qm, _stack_heads(s_hi), NN) + _mm(qm, _stack_heads(s_lo), NN)
                    y_ref[b, c * CHUNK:(c + 1) * CHUNK, p * LANES:(p + 1) * LANES] = (
                        out[:CHUNK] + y0_ref[b, p, c].astype(F32)).astype(y_ref.dtype)
                    s_ref[n] = out[CHUNK:] + g_ref[b, p, c].astype(F32)

    @pl.when(j == 0)
    def _():
        run(yc_ref)

    @pl.when(j > 0)
    def _():
        run(yl_ref)


def _wkv_scan(m, g, q, y0, nb, t_ctx, t_lat, rev):
    bt = t_ctx
    cpb = bt // CHUNK
    nlb = t_lat // bt
    if rev:
        blkmap = lambda j: (0, 0, jnp.where(j == 0, 0, nlb + 1 - j), 0, 0)
        latmap = lambda j: (0, jnp.where(j == 0, nlb - 1, nlb - j), 0)
    else:
        blkmap = lambda j: (0, 0, j, 0, 0)
        latmap = lambda j: (0, jnp.maximum(j - 1, 0), 0)
    return pl.pallas_call(
        functools.partial(_wkv_scan_kernel, rev=rev, cpb=cpb, nb=nb),
        grid=(nlb + 1,),
        in_specs=[pl.BlockSpec((nb, N_PAIRS, cpb, CHUNK, LANES), blkmap)] * 4,
        out_specs=[pl.BlockSpec((nb, bt, D_MODEL), lambda j: (0, 0, 0)), pl.BlockSpec((nb, bt, D_MODEL), latmap)],
        out_shape=[jax.ShapeDtypeStruct((nb, t_ctx, D_MODEL), BF16), jax.ShapeDtypeStruct((nb, t_lat, D_MODEL), BF16)],
        scratch_shapes=[pltpu.VMEM((nb * N_PAIRS, HEAD_DIM, LANES), F32)],
        compiler_params=_cparams(("arbitrary",)),
        name="wkv_scan_bwd" if rev else "wkv_scan_fwd",
    )(m, g, q, y0)


def _wkv(ctx_ops, lat_ops, nb, t_ctx, t_lat, rev):
    pre = _wkv_pre(ctx_ops, lat_ops, nb, t_ctx, t_lat, rev)
    return _wkv_scan(*pre, nb, t_ctx, t_lat, rev)


def _rwkv_out_kernel(yf_ref, yb_ref, bonus_ref, g_ref, lnw_ref, lnb_ref, e_ref, et_ref, wout_ref, x_ref, gpost_ref,
                     gt_ref, o_ref):
    e = e_ref[...]
    et = et_ref[...]
    y = yf_ref[...].astype(F32) + yb_ref[...].astype(F32)
    y_hi, y_lo = _split_bf16(y)
    mean = _seg_bcast((_dot(y_hi, e) + _dot(y_lo, e)) * (1.0 / HEAD_DIM), et, split=True)
    d = y - mean
    var = _seg_sum(d * d, e) * (1.0 / HEAD_DIM)
    rstd = _seg_bcast(lax.rsqrt(var + GN_EPS), et)
    yn = d * rstd * lnw_ref[...] + lnb_ref[...]
    t = ((yn + bonus_ref[...].astype(F32)) * g_ref[...].astype(F32)).astype(BF16)
    o = _dot(t, wout_ref[...])
    o_ref[...] = x_ref[...] + gt_ref[0] * _rmsnorm(o, gpost_ref[...])


def _rwkv_out(yf, yb, bonus, g, ln_w, ln_b, e, et, w_out, x, g_post, gt, t_len):
    rows = x.shape[0]
    tm = min(256, t_len)
    tps = t_len // tm
    nbm = gt.shape[0]
    modmap = (lambda i: (i // tps, 0, 0)) if nbm > 1 else (lambda i: (0, 0, 0))
    const2 = lambda i: (0, 0)
    rowmap = lambda i: (i, 0)
    rowspec = pl.BlockSpec((tm, D_MODEL), rowmap)
    vec = pl.BlockSpec((1, D_MODEL), const2)
    return pl.pallas_call(
        _rwkv_out_kernel,
        grid=(rows // tm,),
        in_specs=[rowspec, rowspec, rowspec, rowspec, vec, vec,
                  pl.BlockSpec((D_MODEL, LANES), const2), pl.BlockSpec((LANES, D_MODEL), const2),
                  pl.BlockSpec((D_MODEL, D_MODEL), const2), rowspec, vec, pl.BlockSpec((1, 1, D_MODEL), modmap)],
        out_specs=rowspec,
        out_shape=jax.ShapeDtypeStruct((rows, D_MODEL), F32),
        compiler_params=_cparams(("parallel",)),
        name="rwkv_out",
    )(yf, yb, bonus, g, ln_w, ln_b, e, et, w_out, x, g_post, gt)


def kernel(x, c, ctx, c_ctx, mod_w, mod_b, norm_pre, norm_post, f_w_in, f_w_mix, f_b_mix, f_w_out,
           r_mu, r_w_in, r_w0, r_w1, r_w2, r_a0, r_a1, r_a2, r_v0, r_v1, r_v2,
           r_k_k, r_k_a, r_r_k, r_ln_w, r_ln_b, r_w_out):
    nb, t_lat, d = x.shape
    t_ctx = ctx.shape[1]
    assert d == D_MODEL and t_lat % t_ctx == 0 and t_ctx % CHUNK == 0 and t_lat % GRID_W == 0

    cond_rows = 8
    cond = jnp.zeros((cond_rows, d), F32).at[:nb].set(c).at[nb].set(c_ctx)
    mods = _modulation(cond, mod_w, mod_b)

    e, et = _head_indicator()
    tabs_lat = _dft_tables(t_lat)
    tabs_ctx = _dft_tables(t_ctx)

    def fourier_mix(xr, t_len, tabs, sc, sh, gt, g_pre, g_post, w_in, wcs, b_mix, w_out):
        p, sz = _fourier_in(xr, t_len, g_pre, sc, sh, w_in, wcs)
        return _fourier_out(*tabs, _dft_fold(p, t_len), p, sz, b_mix, w_out, xr, g_post, gt, t_len)

    xl = x.reshape(nb * t_lat, d)
    xc = ctx.reshape(nb * t_ctx, d)
    v_first = None
    vec = lambda a: a.reshape(1, d)

    for i in range(DEPTH):
        last = i == DEPTH - 1
        kind, j = i % 2, i // 2
        m = mods[i]
        sh_l, sc_l, gt_l = (m[:nb, q * d:(q + 1) * d].reshape(nb, 1, d) for q in range(3))
        sh_c, sc_c, gt_c = (m[nb:nb + 1, q * d:(q + 1) * d].reshape(1, 1, d) for q in range(3))
        g_pre, g_post = vec(norm_pre[i]), vec(norm_post[i])
        if kind == 0:
            w_in = f_w_in[j].astype(BF16)
            w_out = f_w_out[j].astype(BF16)
            wcs = _fold_channel_dft(f_w_mix[j])
            b_mix = vec(f_b_mix[j])
            xl_new = fourier_mix(xl, t_lat, tabs_lat, sc_l, sh_l, gt_l, g_pre, g_post, w_in, wcs, b_mix, w_out)
            if not last:
                xc = fourier_mix(xc, t_ctx, tabs_ctx, sc_c, sh_c, gt_c, g_pre, g_post, w_in, wcs, b_mix, w_out)
            xl = xl_new
        else:
            wts = dict(
                mu=r_mu[j], w4=r_w_in[j].astype(BF16),
                w1=jnp.concatenate([r_w1[j, 0], r_w1[j, 1]], axis=1).astype(BF16), w2=r_w2[j].astype(BF16), w0=r_w0[j],
                a1=jnp.concatenate([r_a1[j, 0], r_a1[j, 1]], axis=1).astype(BF16), a2=r_a2[j].astype(BF16), a0=r_a0[j],
                k_k=vec(r_k_k[j]), k_a=vec(r_k_a[j]), r_k=vec(r_r_k[j]), e=e, et=et)
            if j > 0:
                wts.update(v1=r_v1[j - 1].astype(BF16), v2=r_v2[j - 1].astype(BF16), v0=vec(r_v0[j - 1]))
            fc = _rwkv_features(xc, t_ctx, False, g_pre, sc_c, sh_c, wts, None if j == 0 else v_first[0])
            fl = _rwkv_features(xl, t_lat, True, g_pre, sc_l, sh_l, wts, None if j == 0 else v_first[1])
            if j == 0:
                v_first = (fc[1], fl[1])
            ys = []
            for n, rev in enumerate((False, True)):
                pick = lambda f, t: tuple(a.reshape(nb, t, d) for a in (f[0], f[3 + n], f[1], f[2], f[7 + n], f[5 + n]))
                ys.append(_wkv(pick(fc, t_ctx), pick(fl, t_lat), nb, t_ctx, t_lat, rev))
            w_out = r_w_out[j].astype(BF16)
            ln_w, ln_b = vec(r_ln_w[j]), vec(r_ln_b[j])
            xl_new = _rwkv_out(ys[0][1].reshape(-1, d), ys[1][1].reshape(-1, d), fl[10], fl[9], ln_w, ln_b, e, et,
                               w_out, xl, g_post, gt_l, t_lat)
            if not last:
                xc = _rwkv_out(ys[0][0].reshape(-1, d), ys[1][0].reshape(-1, d), fc[10], fc[9], ln_w, ln_b, e, et,
                               w_out, xc, g_post, gt_c, t_ctx)
            xl = xl_new
    return xl.reshape(nb, t_lat, d)
```

```python
import functools
import math

import numpy as np
import jax
import jax.numpy as jnp
from jax import lax
from jax.experimental import pallas as pl
from jax.experimental.pallas import tpu as pltpu

F32 = jnp.float32
BF16 = jnp.bfloat16
HIGHEST = lax.Precision.HIGHEST

D_MODEL = 1024
DEPTH = 4
GRID_W = 64
HEAD_DIM = 64
N_HEADS = D_MODEL // HEAD_DIM
N_FGROUPS = 8
FGROUP_DIM = D_MODEL // N_FGROUPS
LORA_DECAY = 64
LORA_ICLR = 64
LORA_VRES = 32
RMS_EPS = 1e-6
GN_EPS = 64e-5

LANES = 128
CHUNK = 64
HEADS_PER_TILE = LANES // HEAD_DIM
N_PAIRS = D_MODEL // LANES
VMEM_LIMIT = 56 * 1024 * 1024


def _cparams(sem):
    return pltpu.CompilerParams(dimension_semantics=sem, vmem_limit_bytes=VMEM_LIMIT)


def _dot(a, b):
    return jnp.dot(a, b, preferred_element_type=F32)


def _dot_hi(a, b):
    return jnp.dot(a, b, preferred_element_type=F32, precision=HIGHEST)


def _rmsnorm(x, g):
    return x * lax.rsqrt(jnp.mean(x * x, axis=-1, keepdims=True) + RMS_EPS) * g


def _split_bf16(x):
    hi = x.astype(BF16)
    lo = (x - hi.astype(F32)).astype(BF16)
    return hi, lo


def _seg_sum(x, e):
    return _dot(x.astype(BF16), e)


def _seg_bcast(s, et, split=False):
    if not split:
        return _dot(s.astype(BF16), et)
    hi, lo = _split_bf16(s)
    return _dot(hi, et) + _dot(lo, et)


def _head_indicator():
    e = np.zeros((D_MODEL, LANES), np.float32)
    e[np.arange(D_MODEL), np.arange(D_MODEL) // HEAD_DIM] = 1.0
    return jnp.asarray(e, BF16), jnp.asarray(e.T, BF16)


def _mod_kernel(cond_ref, w_ref, b_ref, o_ref):
    cond = cond_ref[...]
    s = cond * jax.nn.sigmoid(cond)
    o_ref[0] = _dot_hi(s, w_ref[0]) + b_ref[0]


def _modulation(cond, mod_w, mod_b):
    rows = cond.shape[0]
    tn = 1536
    return pl.pallas_call(
        _mod_kernel,
        grid=(DEPTH, 3 * D_MODEL // tn),
        in_specs=[
            pl.BlockSpec((rows, D_MODEL), lambda i, j: (0, 0)),
            pl.BlockSpec((1, D_MODEL, tn), lambda i, j: (i, 0, j)),
            pl.BlockSpec((1, 1, tn), lambda i, j: (i, 0, j)),
        ],
        out_specs=pl.BlockSpec((1, rows, tn), lambda i, j: (i, 0, j)),
        out_shape=jax.ShapeDtypeStruct((DEPTH, rows, 3 * D_MODEL), F32),
        compiler_params=_cparams(("parallel", "parallel")),
        name="modulation",
    )(cond, mod_w, mod_b.reshape(DEPTH, 1, 3 * D_MODEL))


def _wcs_kernel(cc_ref, sc_ref, w_ref, o_ref):
    w = w_ref[0]
    o_ref[0, :, :FGROUP_DIM] = _dot_hi(cc_ref[...], w).astype(o_ref.dtype)
    o_ref[0, :, FGROUP_DIM:] = _dot_hi(sc_ref[...], w).astype(o_ref.dtype)


def _fold_channel_dft(w_mix):
    n = np.arange(FGROUP_DIM)
    ang = 2.0 * np.pi * ((n[:, None] * n[None, :]) % FGROUP_DIM) / FGROUP_DIM
    cc = jnp.asarray(np.cos(ang), F32)
    sc = jnp.asarray(np.sin(ang), F32)
    return pl.pallas_call(
        _wcs_kernel,
        grid=(N_FGROUPS,),
        in_specs=[
            pl.BlockSpec((FGROUP_DIM, FGROUP_DIM), lambda g: (0, 0)),
            pl.BlockSpec((FGROUP_DIM, FGROUP_DIM), lambda g: (0, 0)),
            pl.BlockSpec((1, FGROUP_DIM, FGROUP_DIM), lambda g: (g, 0, 0)),
        ],
        out_specs=pl.BlockSpec((1, FGROUP_DIM, 2 * FGROUP_DIM), lambda g: (g, 0, 0)),
        out_shape=jax.ShapeDtypeStruct((N_FGROUPS, FGROUP_DIM, 2 * FGROUP_DIM), BF16),
        compiler_params=_cparams(("parallel",)),
        name="fold_channel_dft",
    )(cc, sc, w_mix)


DFT_TILE = 512
DFT_TILE_PAD = 16


def _dft_tile(t_len):
    return min(DFT_TILE, t_len // 2)


def _dft_table_kernel(ca_ref, sa_ref, cb_ref, sb_ref, c_ref, s_ref):
    ca = ca_ref[0]
    sa = sa_ref[0]
    cb = cb_ref[...]
    sb = sb_ref[...]
    c_ref[0] = (ca * cb - sa * sb).astype(c_ref.dtype)
    s_ref[0] = (sa * cb + ca * sb).astype(s_ref.dtype)


def _angles(p, t, period):
    return (2.0 * math.pi / period) * ((p[:, None] * t[None, :]) % period).astype(F32)


def _dft_tables(t_len):
    half = t_len // 2
    tm = _dft_tile(t_len)
    rows = tm + DFT_TILE_PAD
    nti = half // tm
    tt = min(2048, half)
    t = jnp.arange(half, dtype=jnp.int32)
    ang_a = _angles(jnp.arange(nti, dtype=jnp.int32) * tm, t, t_len)
    ang_b = _angles(jnp.arange(rows, dtype=jnp.int32), t, t_len)
    ca = jnp.cos(ang_a).reshape(nti, 1, half)
    sa = jnp.sin(ang_a).reshape(nti, 1, half)
    cb = jnp.cos(ang_b)
    sb = jnp.sin(ang_b)
    return pl.pallas_call(
        _dft_table_kernel,
        grid=(half // tt, nti),
        in_specs=[
            pl.BlockSpec((1, 1, tt), lambda j, i: (i, 0, j)),
            pl.BlockSpec((1, 1, tt), lambda j, i: (i, 0, j)),
            pl.BlockSpec((rows, tt), lambda j, i: (0, j)),
            pl.BlockSpec((rows, tt), lambda j, i: (0, j)),
        ],
        out_specs=[
            pl.BlockSpec((1, rows, tt), lambda j, i: (i, 0, j)),
            pl.BlockSpec((1, rows, tt), lambda j, i: (i, 0, j)),
        ],
        out_shape=[jax.ShapeDtypeStruct((nti, rows, half), BF16)] * 2,
        compiler_params=_cparams(("parallel", "parallel")),
        name="dft_tables",
    )(ca, sa, cb, sb)


def _mirror_select(n_out, n_in, offset):
    i = lax.broadcasted_iota(jnp.int32, (n_out, n_in), 0)
    c = lax.broadcasted_iota(jnp.int32, (n_out, n_in), 1)
    return (c == offset - i).astype(BF16)


def _dft_fold_kernel(a_ref, m_ref, x_ref, o_ref):
    tf = a_ref.shape[0]
    a = a_ref[...].astype(F32)
    src = jnp.concatenate([m_ref[...], x_ref[...]], axis=0)
    mir = _dot(_mirror_select(tf, tf + DFT_TILE_PAD, tf), src)
    t_is_zero = (lax.broadcasted_iota(jnp.int32, (tf, 1), 0) == 0) & (pl.program_id(1) == 0)
    o_ref[:, :D_MODEL] = jnp.where(t_is_zero, a[:, :D_MODEL], a[:, :D_MODEL] + mir[:, :D_MODEL]).astype(o_ref.dtype)
    o_ref[:, D_MODEL:] = (a[:, D_MODEL:] - mir[:, D_MODEL:]).astype(o_ref.dtype)


def _dft_fold(p, t_len):
    rows = p.shape[0]
    nb = rows // t_len
    half = t_len // 2
    tf = _dft_tile(t_len)
    ntf = t_len // tf
    pad = DFT_TILE_PAD
    width = p.shape[1]
    return pl.pallas_call(
        _dft_fold_kernel,
        grid=(nb, half // tf),
        in_specs=[
            pl.BlockSpec((tf, width), lambda b, j: (b * ntf + j, 0)),
            pl.BlockSpec((tf, width), lambda b, j: (b * ntf + ntf - 1 - j, 0)),
            pl.BlockSpec((pad, width), lambda b, j: (b * (t_len // pad) + jnp.where(j == 0, 0, (t_len - j * tf) // pad), 0)),
        ],
        out_specs=pl.BlockSpec((tf, width), lambda b, j: (b * (half // tf) + j, 0)),
        out_shape=jax.ShapeDtypeStruct((nb * half, width), BF16),
        compiler_params=_cparams(("parallel", "parallel")),
        name="dft_fold",
    )(p, p, p)


def _fourier_in_kernel(x_ref, g_ref, sc_ref, sh_ref, win_ref, wcs_ref, p_ref, sz_ref):
    h = _rmsnorm(x_ref[...], g_ref[...]) * (1.0 + sc_ref[0]) + sh_ref[0]
    uz = _dot(h.astype(BF16), win_ref[...])
    z = uz[:, D_MODEL:]
    sz_ref[...] = (z * jax.nn.sigmoid(z)).astype(sz_ref.dtype)
    for g in range(N_FGROUPS):
        lo, hi = g * FGROUP_DIM, (g + 1) * FGROUP_DIM
        pg = _dot(uz[:, lo:hi].astype(BF16), wcs_ref[g])
        p_ref[:, lo:hi] = pg[:, :FGROUP_DIM].astype(p_ref.dtype)
        p_ref[:, D_MODEL + lo:D_MODEL + hi] = pg[:, FGROUP_DIM:].astype(p_ref.dtype)


def _fourier_in(x, t_len, g_pre, sc, sh, w_in, wcs):
    rows = x.shape[0]
    tm = min(512, t_len)
    tpb = t_len // tm
    nbm = sc.shape[0]
    modmap = (lambda i: (i // tpb, 0, 0)) if nbm > 1 else (lambda i: (0, 0, 0))
    return pl.pallas_call(
        _fourier_in_kernel,
        grid=(rows // tm,),
        in_specs=[
            pl.BlockSpec((tm, D_MODEL), lambda i: (i, 0)),
            pl.BlockSpec((1, D_MODEL), lambda i: (0, 0)),
            pl.BlockSpec((1, 1, D_MODEL), modmap),
            pl.BlockSpec((1, 1, D_MODEL), modmap),
            pl.BlockSpec((D_MODEL, 2 * D_MODEL), lambda i: (0, 0)),
            pl.BlockSpec((N_FGROUPS, FGROUP_DIM, 2 * FGROUP_DIM), lambda i: (0, 0, 0)),
        ],
        out_specs=[
            pl.BlockSpec((tm, 2 * D_MODEL), lambda i: (i, 0)),
            pl.BlockSpec((tm, D_MODEL), lambda i: (i, 0)),
        ],
        out_shape=[
            jax.ShapeDtypeStruct((rows, 2 * D_MODEL), BF16),
            jax.ShapeDtypeStruct((rows, D_MODEL), BF16),
        ],
        compiler_params=_cparams(("parallel",)),
        name="fourier_in",
    )(x, g_pre, sc, sh, w_in, wcs)


def _fourier_out_kernel(c_ref, s_ref, ec_ref, os_ref, ph_ref, szd_ref, szm_ref, bmix_ref, wout_ref, xd_ref, xm_ref,
                        gpost_ref, gt_ref, y_ref, accc_ref, accs_ref, outd_ref, outm_ref, sem, *, scale, tiles_per_seq):
    b, i, k = pl.program_id(0), pl.program_id(1), pl.program_id(2)
    nb, nti = pl.num_programs(0), pl.num_programs(1)
    tm = outd_ref.shape[0]

    @pl.when(k == 0)
    def _():
        accc_ref[...] = jnp.zeros_like(accc_ref)
        accs_ref[...] = jnp.zeros_like(accs_ref)

    accc_ref[...] += _dot(c_ref[0], ec_ref[...])
    accs_ref[...] += _dot(s_ref[0], os_ref[...])

    def out_copies(bb, ii):
        direct = pl.multiple_of((bb * tiles_per_seq + ii) * tm, tm)
        mirror = pl.multiple_of((bb * tiles_per_seq + tiles_per_seq - 1 - ii) * tm, tm)
        return (pltpu.make_async_copy(outd_ref, y_ref.at[pl.ds(direct, tm)], sem.at[0]),
                pltpu.make_async_copy(outm_ref, y_ref.at[pl.ds(mirror, tm)], sem.at[1]))

    def finish(f, sz_ref, x_ref, o_ref):
        gated = ((f * scale + bmix_ref[...]) * sz_ref[...].astype(F32)).astype(BF16)
        o = _dot(gated, wout_ref[...])
        o_ref[...] = x_ref[...] + gt_ref[0] * _rmsnorm(o, gpost_ref[...])

    @pl.when(k == pl.num_programs(2) - 1)
    def _():
        @pl.when((b > 0) | (i > 0))
        def _():
            for cp in out_copies(jnp.where(i == 0, b - 1, b), jnp.where(i == 0, nti - 1, i - 1)):
                cp.wait()

        rows = accc_ref.shape[0]
        p = i * tm + lax.broadcasted_iota(jnp.int32, (rows, 1), 0)
        sign = (1 - 2 * (p & 1)).astype(F32)
        fc = accc_ref[...] + sign * ph_ref[0:1, :].astype(F32)
        fs = accs_ref[...]
        finish((fc - fs)[:tm], szd_ref, xd_ref, outd_ref)
        hi, lo = _split_bf16(fc + fs)
        sel = _mirror_select(tm, rows, tm)
        finish(_dot(sel, hi) + _dot(sel, lo), szm_ref, xm_ref, outm_ref)
        for cp in out_copies(b, i):
            cp.start()

        @pl.when((b == nb - 1) & (i == nti - 1))
        def _():
            for cp in out_copies(b, i):
                cp.wait()


def _fourier_out(c_tab, s_tab, eo, p, sz, b_mix, w_out, x, g_post, gt, t_len):
    rows = x.shape[0]
    nb = rows // t_len
    half = t_len // 2
    tm = _dft_tile(t_len)
    tk = min(1024, half)
    pad = DFT_TILE_PAD
    nti, ntk, ntf = half // tm, half // tk, t_len // tm
    nbm = gt.shape[0]
    modmap = (lambda b, i, k: (b, 0, 0)) if nbm > 1 else (lambda b, i, k: (0, 0, 0))
    direct = lambda b, i, k: (b * ntf + i, 0)
    mirror = lambda b, i, k: (b * ntf + ntf - 1 - i, 0)
    const = lambda b, i, k: (0, 0)
    rowspec = lambda m: pl.BlockSpec((tm, D_MODEL), m)
    scale = 1.0 / math.sqrt(t_len * FGROUP_DIM)
    return pl.pallas_call(
        functools.partial(_fourier_out_kernel, scale=scale, tiles_per_seq=ntf),
        grid=(nb, nti, ntk),
        in_specs=[
            pl.BlockSpec((1, tm + pad, tk), lambda b, i, k: (i, 0, k)),
            pl.BlockSpec((1, tm + pad, tk), lambda b, i, k: (i, 0, k)),
            pl.BlockSpec((tk, D_MODEL), lambda b, i, k: (b * ntk + k, 0)),
            pl.BlockSpec((tk, D_MODEL), lambda b, i, k: (b * ntk + k, 1)),
            pl.BlockSpec((pad, D_MODEL), lambda b, i, k: ((b * t_len + half) // pad, 0)),
            rowspec(direct), rowspec(mirror),
            pl.BlockSpec((1, D_MODEL), const),
            pl.BlockSpec((D_MODEL, D_MODEL), const),
            rowspec(direct), rowspec(mirror),
            pl.BlockSpec((1, D_MODEL), const),
            pl.BlockSpec((1, 1, D_MODEL), modmap),
        ],
        out_specs=pl.BlockSpec(memory_space=pl.ANY),
        out_shape=jax.ShapeDtypeStruct((rows, D_MODEL), F32),
        scratch_shapes=[pltpu.VMEM((tm + pad, D_MODEL), F32)] * 2 + [pltpu.VMEM((tm, D_MODEL), F32)] * 2
        + [pltpu.SemaphoreType.DMA((2,))],
        compiler_params=_cparams(("arbitrary", "arbitrary", "arbitrary")),
        name="fourier_out",
    )(c_tab, s_tab, eo, eo, p, sz, sz, b_mix, w_out, x, x, g_post, gt)


def _rwkv_feat_kernel(*refs, grid_shift, tiles_per_seq, has_vres):
    if grid_shift:
        x_ref, xp_ref, xn_ref = refs[:3]
        refs = refs[3:]
    else:
        x_ref = refs[0]
        refs = refs[1:]
    (g_ref, sc_ref, sh_ref, mu_ref, w4_ref, w1_ref, w2_ref, w0_ref, a1_ref, a2_ref, a0_ref,
     kk_ref, ka_ref, rk_ref, e_ref, et_ref) = refs[:16]
    refs = refs[16:]
    if has_vres:
        vf_ref, v1_ref, v2_ref, v0_ref = refs[:4]
        refs = refs[4:]
    (r_out, v_out, kkn_out, k0_out, k1_out, lw0_out, lw1_out, b0_out, b1_out, g_out, bonus_out) = refs

    g = g_ref[...]
    sc1 = 1.0 + sc_ref[0]
    sh = sh_ref[0]
    tm = x_ref.shape[0]
    h = _rmsnorm(x_ref[...], g) * sc1 + sh
    row = lax.broadcasted_iota(jnp.int32, (tm, 1), 0)
    if grid_shift:
        i = pl.program_id(0)
        first = (i % tiles_per_seq) == 0
        last = (i % tiles_per_seq) == tiles_per_seq - 1
        hp = _rmsnorm(xp_ref[...], g) * sc1 + sh
        hn = _rmsnorm(xn_ref[...], g) * sc1 + sh
        hp = jnp.where(first, 0.0, hp)
        hn = jnp.where(last, 0.0, hn)
        hall = jnp.concatenate([hp, h, hn], axis=0)
        tot = tm + 2 * GRID_W
        up = hall[0:tm]
        down = hall[2 * GRID_W:2 * GRID_W + tm]
        left = pltpu.roll(hall, 1, axis=0)[GRID_W:GRID_W + tm]
        right = pltpu.roll(hall, tot - 1, axis=0)[GRID_W:GRID_W + tm]
        col = row % GRID_W
        left = jnp.where(col == 0, 0.0, left)
        right = jnp.where(col == GRID_W - 1, 0.0, right)
        nbr = (up + down + left + right) * 0.25
    else:
        left = jnp.where(row == 0, 0.0, pltpu.roll(h, 1, axis=0))
        right = jnp.where(row == tm - 1, 0.0, pltpu.roll(h, tm - 1, axis=0))
        nbr = (left + right) * 0.5
    dlt = nbr - h

    def mix(p):
        return h + dlt * mu_ref[p:p + 1, :]

    e = e_ref[...]
    et = et_ref[...]
    tw = jnp.tanh(_dot(mix(4).astype(BF16), w1_ref[...])).astype(BF16)
    ta = _dot(mix(5).astype(BF16), a1_ref[...]).astype(BF16)
    decay_gain = -math.exp(-0.5)
    a_gate = []
    for n, lw_out in enumerate((lw0_out, lw1_out)):
        wz = w0_ref[n:n + 1, :] + _dot(tw[:, n * LORA_DECAY:(n + 1) * LORA_DECAY], w2_ref[n])
        lw_out[...] = decay_gain * jax.nn.sigmoid(wz)
        az = a0_ref[n:n + 1, :] + _dot(ta[:, n * LORA_ICLR:(n + 1) * LORA_ICLR], a2_ref[n])
        a_gate.append(jax.nn.sigmoid(az))

    gz = _dot(mix(3).astype(BF16), w4_ref[3])
    g_out[...] = (gz * jax.nn.sigmoid(gz)).astype(g_out.dtype)

    k = _dot(mix(1).astype(BF16), w4_ref[1])
    kk = k * kk_ref[...]
    n2 = _seg_sum(kk * kk, e)
    inv = 1.0 / jnp.maximum(jnp.sqrt(n2), 1e-12)
    kkn = kk * _seg_bcast(inv, et, split=True)
    kkn_out[...] = kkn.astype(kkn_out.dtype)
    kd = []
    for a, k_out, b_out in zip(a_gate, (k0_out, k1_out), (b0_out, b1_out)):
        kd.append(k * (1.0 + (a - 1.0) * ka_ref[...]))
        k_out[...] = kd[-1].astype(k_out.dtype)
        b_out[...] = (kkn * a).astype(b_out.dtype)

    xv = mix(2).astype(BF16)
    v = _dot(xv, w4_ref[2])
    if has_vres:
        vz = v0_ref[...] + _dot(_dot(xv, v1_ref[...]).astype(BF16), v2_ref[...])
        v = v + (vf_ref[...] - v) * jax.nn.sigmoid(vz)
    v_out[...] = v.astype(v_out.dtype)

    r = _dot(mix(0).astype(BF16), w4_ref[0])
    r_out[...] = r.astype(r_out.dtype)
    rsum = r * (kd[0] + kd[1]) * rk_ref[...]
    bonus_out[...] = (_seg_bcast(_seg_sum(rsum, e), et) * v).astype(bonus_out.dtype)


def _rwkv_features(x, t_len, grid_shift, g_pre, sc, sh, wts, v_first):
    rows = x.shape[0]
    tm = 256
    tps = t_len // tm
    assert grid_shift or tps == 1
    nbm = sc.shape[0]
    modmap = (lambda i: (i // tps, 0, 0)) if nbm > 1 else (lambda i: (0, 0, 0))
    const2 = lambda i: (0, 0)
    const3 = lambda i: (0, 0, 0)
    rowmap = lambda i: (i, 0)
    has_vres = v_first is not None
    hb = tm // GRID_W
    nhalo = rows // GRID_W

    args, specs = [x], [pl.BlockSpec((tm, D_MODEL), rowmap)]
    if grid_shift:
        args += [x, x]
        specs += [
            pl.BlockSpec((GRID_W, D_MODEL), lambda i: (jnp.maximum(i * hb - 1, 0), 0)),
            pl.BlockSpec((GRID_W, D_MODEL), lambda i: (jnp.minimum((i + 1) * hb, nhalo - 1), 0)),
        ]
    args += [g_pre, sc, sh, wts["mu"], wts["w4"], wts["w1"], wts["w2"], wts["w0"], wts["a1"], wts["a2"], wts["a0"],
             wts["k_k"], wts["k_a"], wts["r_k"], wts["e"], wts["et"]]
    specs += [
        pl.BlockSpec((1, D_MODEL), const2),
        pl.BlockSpec((1, 1, D_MODEL), modmap),
        pl.BlockSpec((1, 1, D_MODEL), modmap),
        pl.BlockSpec((6, D_MODEL), const2),
        pl.BlockSpec((4, D_MODEL, D_MODEL), const3),
        pl.BlockSpec((D_MODEL, 2 * LORA_DECAY), const2),
        pl.BlockSpec((2, LORA_DECAY, D_MODEL), const3),
        pl.BlockSpec((2, D_MODEL), const2),
        pl.BlockSpec((D_MODEL, 2 * LORA_ICLR), const2),
        pl.BlockSpec((2, LORA_ICLR, D_MODEL), const3),
        pl.BlockSpec((2, D_MODEL), const2),
        pl.BlockSpec((1, D_MODEL), const2),
        pl.BlockSpec((1, D_MODEL), const2),
        pl.BlockSpec((1, D_MODEL), const2),
        pl.BlockSpec((D_MODEL, LANES), const2),
        pl.BlockSpec((LANES, D_MODEL), const2),
    ]
    if has_vres:
        args += [v_first, wts["v1"], wts["v2"], wts["v0"]]
        specs += [
            pl.BlockSpec((tm, D_MODEL), rowmap),
            pl.BlockSpec((D_MODEL, LORA_VRES), const2),
            pl.BlockSpec((LORA_VRES, D_MODEL), const2),
            pl.BlockSpec((1, D_MODEL), const2),
        ]
    n_out = 11
    out_dtypes = [BF16] * 5 + [F32] * 2 + [BF16] * 4
    return pl.pallas_call(
        functools.partial(_rwkv_feat_kernel, grid_shift=grid_shift, tiles_per_seq=tps, has_vres=has_vres),
        grid=(rows // tm,),
        in_specs=specs,
        out_specs=[pl.BlockSpec((tm, D_MODEL), rowmap)] * n_out,
        out_shape=[jax.ShapeDtypeStruct((rows, D_MODEL), dt) for dt in out_dtypes],
        compiler_params=_cparams(("parallel",)),
        name="rwkv_features",
    )(*args)


NN = (((1,), (0,)), ((), ()))
NT = (((1,), (1,)), ((), ()))


def _mm(a, b, dn):
    return lax.dot_general(a.astype(BF16), b.astype(BF16), dn, preferred_element_type=F32)


def _stack_heads(x):
    first_head = lax.broadcasted_iota(jnp.int32, x.shape, 1) < HEAD_DIM
    zero = jnp.zeros_like(x)
    return jnp.concatenate([jnp.where(first_head, x, zero), jnp.where(first_head, zero, x)], axis=0).astype(BF16)


def _prefix_rows(x, rev):
    n = x.shape[0]
    row = lax.broadcasted_iota(jnp.int32, (n, 1), 0)
    s = 1
    while s < n:
        if rev:
            x = x + jnp.where(row < n - s, pltpu.roll(x, n - s, axis=0), 0.0)
        else:
            x = x + jnp.where(row >= s, pltpu.roll(x, s, axis=0), 0.0)
        s *= 2
    return x


def _heads_to_rows(x):
    return jnp.concatenate([x[:, :HEAD_DIM], x[:, HEAD_DIM:]], axis=0)


def _wkv_pre_units(units, rev):
    cl = CHUNK
    nu = range(len(units))
    rt = lax.broadcasted_iota(jnp.int32, (cl, LANES), 0)
    ct = lax.broadcasted_iota(jnp.int32, (cl, LANES), 1) % cl
    eye = (rt == ct).astype(F32)
    strict = (ct > rt) if rev else (ct < rt)
    incl = (ct >= rt) if rev else (ct <= rt)
    stack = _stack_heads

    cum = [_prefix_rows(u[5], rev) for u in units]
    tot = [jnp.sum(u[5], axis=0, keepdims=True) for u in units]
    a_t = [(-units[i][3] * jnp.exp(cum[i] - units[i][5])).astype(BF16) for i in nu]
    r_t = [(units[i][0] * jnp.exp(cum[i])).astype(BF16) for i in nu]
    e_neg = [jnp.exp(-cum[i]) for i in nu]
    bk_h = [jnp.concatenate([stack(units[i][4] * e_neg[i]), stack(units[i][1] * e_neg[i])], axis=0) for i in nu]
    e_rest = [jnp.exp(tot[i] - cum[i]) for i in nu]
    b_r = [units[i][4] * e_rest[i] for i in nu]
    k_r = [units[i][1] * e_rest[i] for i in nu]
    v_s = [stack(u[2]) for u in units]

    scores = [_mm(jnp.concatenate([a_t[i], r_t[i]], axis=0), bk_h[i], NT) for i in nu]
    a_ab = [jnp.where(strict, sc[:cl, :LANES], 0.0) for sc in scores]
    a_ak = [jnp.where(strict, sc[:cl, LANES:], 0.0).astype(BF16) for sc in scores]
    a_r = [jnp.concatenate([jnp.where(incl, sc[cl:, :LANES], 0.0), jnp.where(incl, sc[cl:, LANES:], 0.0)],
                           axis=1).astype(BF16) for sc in scores]

    def same_block(s):
        return (rt // s) == (ct // s)

    inv = [eye + jnp.where(same_block(2), x, 0.0) for x in a_ab]
    s = 2
    while s < cl:
        off = same_block(2 * s) & jnp.logical_not(same_block(s))
        xs = [_mm(jnp.where(off, a_ab[i], 0.0), stack(inv[i]), NN) for i in nu]
        inv = [inv[i] + _mm(inv[i], stack(xs[i]), NN) for i in nu]
        s *= 2

    x2 = [_mm(a_ak[i], v_s[i], NN) for i in nu]
    tu = [_mm(inv[i], jnp.concatenate([stack(a_t[i]), stack(x2[i])], axis=1), NN) for i in nu]
    uh = [t[:, :LANES] for t in tu]
    u0 = [t[:, LANES:] for t in tu]
    uh_s = [stack(x) for x in uh]
    u0v_s = [jnp.concatenate([stack(u0[i]), v_s[i]], axis=0) for i in nu]
    bk_t = [jnp.concatenate([_heads_to_rows(b_r[i]), _heads_to_rows(k_r[i])], axis=0).T.astype(BF16) for i in nu]
    mq = [_mm(jnp.concatenate([bk_t[i][:, :LANES], a_r[i][:, :LANES]], axis=0), uh_s[i], NN) for i in nu]
    gy = [_mm(jnp.concatenate([bk_t[i], a_r[i]], axis=0), u0v_s[i], NN) for i in nu]
    return [(eye * jnp.exp(tot[i]) + mq[i][:cl], gy[i][:cl], r_t[i].astype(F32) + mq[i][cl:], gy[i][cl:]) for i in nu]


def _wkv_pre_kernel(rc, kc, vc, ac, bc, wc, rl, kl, vl, al, bl, wl, m_ref, g_ref, q_ref, y0_ref, *, rev, cpb, ppb):
    j = pl.program_id(2)

    def run(src):
        ids = [(p, c) for p in range(ppb) for c in range(cpb)]
        units = [tuple(ref[0, pl.ds(c * CHUNK, CHUNK), pl.ds(p * LANES, LANES)] for ref in src) for p, c in ids]
        for (p, c), (m, g, q, y0) in zip(ids, _wkv_pre_units(units, rev)):
            m_ref[0, p, c] = m.astype(m_ref.dtype)
            g_ref[0, p, c] = g.astype(g_ref.dtype)
            q_ref[0, p, c] = q.astype(q_ref.dtype)
            y0_ref[0, p, c] = y0.astype(y0_ref.dtype)

    @pl.when(j == 0)
    def _():
        run((rc, kc, vc, ac, bc, wc))

    @pl.when(j > 0)
    def _():
        run((rl, kl, vl, al, bl, wl))


WKV_PAIRS_PER_BLOCK = 8


def _wkv_pre(ctx_ops, lat_ops, nb, t_ctx, t_lat, rev):
    bt = t_ctx
    cpb = bt // CHUNK
    ppb = WKV_PAIRS_PER_BLOCK
    nlb = t_lat // bt
    nch = (t_ctx + t_lat) // CHUNK
    latmap = lambda b, p, j: (b, jnp.maximum(j - 1, 0), p)
    ctxmap = lambda b, p, j: (b, 0, p)
    outmap = lambda b, p, j: (b, p, j, 0, 0)
    blk = (1, bt, ppb * LANES)
    return pl.pallas_call(
        functools.partial(_wkv_pre_kernel, rev=rev, cpb=cpb, ppb=ppb),
        grid=(nb, N_PAIRS // ppb, nlb + 1),
        in_specs=[pl.BlockSpec(blk, ctxmap)] * 6 + [pl.BlockSpec(blk, latmap)] * 6,
        out_specs=[pl.BlockSpec((1, ppb, cpb, CHUNK, LANES), outmap)] * 4,
        out_shape=[jax.ShapeDtypeStruct((nb, N_PAIRS, nch, CHUNK, LANES), BF16)] * 4,
        compiler_params=_cparams(("parallel", "parallel", "parallel")),
        name="wkv_pre_bwd" if rev else "wkv_pre_fwd",
    )(*ctx_ops, *lat_ops)


def _wkv_scan_kernel(m_ref, g_ref, q_ref, y0_ref, yc_ref, yl_ref, s_ref, *, rev, cpb, nb):
    j = pl.program_id(0)

    @pl.when(j == 0)
    def _():
        s_ref[...] = jnp.zeros_like(s_ref)

    def run(y_ref):
        order = range(cpb - 1, -1, -1) if rev else range(cpb)
        for c in order:
            for b in range(nb):
                for p in range(N_PAIRS):
                    n = b * N_PAIRS + p
                    s = s_ref[n]
                    s_hi = s.astype(BF16)
                    s_lo = (s - s_hi.astype(F32)).astype(BF16)
                    qm = jnp.concatenate([q_ref[b, p, c], m_ref[b, p, c]], axis=0)
                    out = _mm(qm, _stack_heads(s_hi), NN) + _mm(qm, _stack_heads(s_lo), NN)
                    y_ref[b, c * CHUNK:(c + 1) * CHUNK, p * LANES:(p + 1) * LANES] = (
                        out[:CHUNK] + y0_ref[b, p, c].astype(F32)).astype(y_ref.dtype)
                    s_ref[n] = out[CHUNK:] + g_ref[b, p, c].astype(F32)

    @pl.when(j == 0)
    def _():
        run(yc_ref)

    @pl.when(j > 0)
    def _():
        run(yl_ref)


def _wkv_scan(m, g, q, y0, nb, t_ctx, t_lat, rev):
    bt = t_ctx
    cpb = bt // CHUNK
    nlb = t_lat // bt
    if rev:
        blkmap = lambda j: (0, 0, jnp.where(j == 0, 0, nlb + 1 - j), 0, 0)
        latmap = lambda j: (0, jnp.where(j == 0, nlb - 1, nlb - j), 0)
    else:
        blkmap = lambda j: (0, 0, j, 0, 0)
        latmap = lambda j: (0, jnp.maximum(j - 1, 0), 0)
    return pl.pallas_call(
        functools.partial(_wkv_scan_kernel, rev=rev, cpb=cpb, nb=nb),
        grid=(nlb + 1,),
        in_specs=[pl.BlockSpec((nb, N_PAIRS, cpb, CHUNK, LANES), blkmap)] * 4,
        out_specs=[pl.BlockSpec((nb, bt, D_MODEL), lambda j: (0, 0, 0)), pl.BlockSpec((nb, bt, D_MODEL), latmap)],
        out_shape=[jax.ShapeDtypeStruct((nb, t_ctx, D_MODEL), BF16), jax.ShapeDtypeStruct((nb, t_lat, D_MODEL), BF16)],
        scratch_shapes=[pltpu.VMEM((nb * N_PAIRS, HEAD_DIM, LANES), F32)],
        compiler_params=_cparams(("arbitrary",)),
        name="wkv_scan_bwd" if rev else "wkv_scan_fwd",
    )(m, g, q, y0)


def _wkv(ctx_ops, lat_ops, nb, t_ctx, t_lat, rev):
    pre = _wkv_pre(ctx_ops, lat_ops, nb, t_ctx, t_lat, rev)
    return _wkv_scan(*pre, nb, t_ctx, t_lat, rev)


def _rwkv_out_kernel(yf_ref, yb_ref, bonus_ref, g_ref, lnw_ref, lnb_ref, e_ref, et_ref, wout_ref, x_ref, gpost_ref,
                     gt_ref, o_ref):
    e = e_ref[...]
    et = et_ref[...]
    y = yf_ref[...].astype(F32) + yb_ref[...].astype(F32)
    y_hi, y_lo = _split_bf16(y)
    mean = _seg_bcast((_dot(y_hi, e) + _dot(y_lo, e)) * (1.0 / HEAD_DIM), et, split=True)
    d = y - mean
    var = _seg_sum(d * d, e) * (1.0 / HEAD_DIM)
    rstd = _seg_bcast(lax.rsqrt(var + GN_EPS), et)
    yn = d * rstd * lnw_ref[...] + lnb_ref[...]
    t = ((yn + bonus_ref[...].astype(F32)) * g_ref[...].astype(F32)).astype(BF16)
    o = _dot(t, wout_ref[...])
    o_ref[...] = x_ref[...] + gt_ref[0] * _rmsnorm(o, gpost_ref[...])


def _rwkv_out(yf, yb, bonus, g, ln_w, ln_b, e, et, w_out, x, g_post, gt, t_len):
    rows = x.shape[0]
    tm = min(256, t_len)
    tps = t_len // tm
    nbm = gt.shape[0]
    modmap = (lambda i: (i // tps, 0, 0)) if nbm > 1 else (lambda i: (0, 0, 0))
    const2 = lambda i: (0, 0)
    rowmap = lambda i: (i, 0)
    rowspec = pl.BlockSpec((tm, D_MODEL), rowmap)
    vec = pl.BlockSpec((1, D_MODEL), const2)
    return pl.pallas_call(
        _rwkv_out_kernel,
        grid=(rows // tm,),
        in_specs=[rowspec, rowspec, rowspec, rowspec, vec, vec,
                  pl.BlockSpec((D_MODEL, LANES), const2), pl.BlockSpec((LANES, D_MODEL), const2),
                  pl.BlockSpec((D_MODEL, D_MODEL), const2), rowspec, vec, pl.BlockSpec((1, 1, D_MODEL), modmap)],
        out_specs=rowspec,
        out_shape=jax.ShapeDtypeStruct((rows, D_MODEL), F32),
        compiler_params=_cparams(("parallel",)),
        name="rwkv_out",
    )(yf, yb, bonus, g, ln_w, ln_b, e, et, w_out, x, g_post, gt)


def kernel(x, c, ctx, c_ctx, mod_w, mod_b, norm_pre, norm_post, f_w_in, f_w_mix, f_b_mix, f_w_out,
           r_mu, r_w_in, r_w0, r_w1, r_w2, r_a0, r_a1, r_a2, r_v0, r_v1, r_v2,
           r_k_k, r_k_a, r_r_k, r_ln_w, r_ln_b, r_w_out):
    nb, t_lat, d = x.shape
    t_ctx = ctx.shape[1]
    assert d == D_MODEL and t_lat % t_ctx == 0 and t_ctx % CHUNK == 0 and t_lat % GRID_W == 0

    cond_rows = 8
    cond = jnp.zeros((cond_rows, d), F32).at[:nb].set(c).at[nb].set(c_ctx)
    mods = _modulation(cond, mod_w, mod_b)

    e, et = _head_indicator()
    tabs_lat = _dft_tables(t_lat)
    tabs_ctx = _dft_tables(t_ctx)

    def fourier_mix(xr, t_len, tabs, sc, sh, gt, g_pre, g_post, w_in, wcs, b_mix, w_out):
        p, sz = _fourier_in(xr, t_len, g_pre, sc, sh, w_in, wcs)
        return _fourier_out(*tabs, _dft_fold(p, t_len), p, sz, b_mix, w_out, xr, g_post, gt, t_len)

    xl = x.reshape(nb * t_lat, d)
    xc = ctx.reshape(nb * t_ctx, d)
    v_first = None
    vec = lambda a: a.reshape(1, d)

    for i in range(DEPTH):
        last = i == DEPTH - 1
        kind, j = i % 2, i // 2
        m = mods[i]
        sh_l, sc_l, gt_l = (m[:nb, q * d:(q + 1) * d].reshape(nb, 1, d) for q in range(3))
        sh_c, sc_c, gt_c = (m[nb:nb + 1, q * d:(q + 1) * d].reshape(1, 1, d) for q in range(3))
        g_pre, g_post = vec(norm_pre[i]), vec(norm_post[i])
        if kind == 0:
            w_in = f_w_in[j].astype(BF16)
            w_out = f_w_out[j].astype(BF16)
            wcs = _fold_channel_dft(f_w_mix[j])
            b_mix = vec(f_b_mix[j])
            xl_new = fourier_mix(xl, t_lat, tabs_lat, sc_l, sh_l, gt_l, g_pre, g_post, w_in, wcs, b_mix, w_out)
            if not last:
                xc = fourier_mix(xc, t_ctx, tabs_ctx, sc_c, sh_c, gt_c, g_pre, g_post, w_in, wcs, b_mix, w_out)
            xl = xl_new
        else:
            wts = dict(
                mu=r_mu[j], w4=r_w_in[j].astype(BF16),
                w1=jnp.concatenate([r_w1[j, 0], r_w1[j, 1]], axis=1).astype(BF16), w2=r_w2[j].astype(BF16), w0=r_w0[j],
                a1=jnp.concatenate([r_a1[j, 0], r_a1[j, 1]], axis=1).astype(BF16), a2=r_a2[j].astype(BF16), a0=r_a0[j],
                k_k=vec(r_k_k[j]), k_a=vec(r_k_a[j]), r_k=vec(r_r_k[j]), e=e, et=et)
            if j > 0:
                wts.update(v1=r_v1[j - 1].astype(BF16), v2=r_v2[j - 1].astype(BF16), v0=vec(r_v0[j - 1]))
            fc = _rwkv_features(xc, t_ctx, False, g_pre, sc_c, sh_c, wts, None if j == 0 else v_first[0])
            fl = _rwkv_features(xl, t_lat, True, g_pre, sc_l, sh_l, wts, None if j == 0 else v_first[1])
            if j == 0:
                v_first = (fc[1], fl[1])
            ys = []
            for n, rev in enumerate((False, True)):
                pick = lambda f, t: tuple(a.reshape(nb, t, d) for a in (f[0], f[3 + n], f[1], f[2], f[7 + n], f[5 + n]))
                ys.append(_wkv(pick(fc, t_ctx), pick(fl, t_lat), nb, t_ctx, t_lat, rev))
            w_out = r_w_out[j].astype(BF16)
            ln_w, ln_b = vec(r_ln_w[j]), vec(r_ln_b[j])
            xl_new = _rwkv_out(ys[0][1].reshape(-1, d), ys[1][1].reshape(-1, d), fl[10], fl[9], ln_w, ln_b, e, et,
                               w_out, xl, g_post, gt_l, t_lat)
            if not last:
                xc = _rwkv_out(ys[0][0].reshape(-1, d), ys[1][0].reshape(-1, d), fc[10], fc[9], ln_w, ln_b, e, et,
                               w_out, xc, g_post, gt_c, t_ctx)
            xl = xl_new
    return xl.reshape(nb, t_lat, d)
```
